```python
import jax, jax.numpy as jnp
from jax import lax
import numpy as np

D_MODEL = 1024
BATCH = 1
SEQ = 16384
DEPTH = 4
DEC_BATCH = 4
DEC_SEQ = 8192
PAST_LEN = 128

N_META = 16
N_MIXERS = 2
SHORT_CONV_W = 3
CONFORMER_CONV_W = 31
D_FF = 2816
N_EXPERTS = 8
TOP_K = 2
D_FF_EXPERT = 3584
N_EVEN = (DEPTH + 1) // 2
N_ODD = DEPTH // 2
DEEPNORM_ALPHA = (2.0 * DEPTH) ** 0.25
DEEPNORM_BETA = (8.0 * DEPTH) ** -0.25
LN_EPS = 1e-5

kernel_name = "hybrid_conv_moe_encoder"


def layer_norm(x, g, b):
    xf = x.astype(jnp.float32)
    mu = jnp.mean(xf, axis=-1, keepdims=True)
    var = jnp.mean(jnp.square(xf - mu), axis=-1, keepdims=True)
    y = (xf - mu) * lax.rsqrt(var + LN_EPS)
    return (y * g.astype(jnp.float32) + b.astype(jnp.float32)).astype(x.dtype)


def depthwise_conv(x, w):
    k = w.shape[0]
    return lax.conv_general_dilated(
        x, w[:, None, :].astype(x.dtype), window_strides=(1,), padding=[(k // 2, k // 2)],
        dimension_numbers=("NWC", "WIO", "NWC"), feature_group_count=x.shape[-1])


def short_conv_mixer(x, w_in, conv_w, w_out):
    bch = jnp.einsum("btd,de->bte", x, w_in)
    b_gate, c_gate, h = jnp.split(bch, 3, axis=-1)
    y = b_gate * depthwise_conv(c_gate * h, conv_w)
    return jnp.einsum("btd,de->bte", y, w_out)


def conformer_conv(x, w_pw1, b_pw1, conv_w, conv_b, ln_g, ln_b, w_pw2, b_pw2):
    u = jnp.einsum("btd,de->bte", x, w_pw1) + b_pw1
    a, g = jnp.split(u, 2, axis=-1)
    u = a * jax.nn.sigmoid(g)
    u = depthwise_conv(u, conv_w) + conv_b
    u = jax.nn.silu(layer_norm(u, ln_g, ln_b))
    return jnp.einsum("btd,de->bte", u, w_pw2) + b_pw2


def swiglu(x, w_gate, w_up, w_down):
    h = jax.nn.silu(jnp.einsum("btd,df->btf", x, w_gate)) * jnp.einsum("btd,df->btf", x, w_up)
    return jnp.einsum("btf,fd->btd", h, w_down)


def moe_swiglu(x, router, w_gate, w_up, w_down):
    logits = jnp.einsum("btd,de->bte", x.astype(jnp.float32), router.astype(jnp.float32))
    top_v, top_i = lax.top_k(logits, TOP_K)
    top_w = jax.nn.softmax(top_v, axis=-1)
    gate = jnp.sum(jax.nn.one_hot(top_i, N_EXPERTS, dtype=jnp.float32) * top_w[..., None], axis=-2)
    gate = gate.astype(x.dtype)
    out = jnp.zeros_like(x)
    for e in range(N_EXPERTS):
        out = out + gate[..., e:e + 1] * swiglu(x, w_gate[e], w_up[e], w_down[e])
    return out


def encode(x, meta_tokens,
           a_w_in, a_conv_w, a_w_out,
           b_w_pw1, b_b_pw1, b_conv_w, b_conv_b, b_ln_g, b_ln_b, b_w_pw2, b_b_pw2,
           ffn_w_gate, ffn_w_up, ffn_w_down,
           moe_router, moe_w_gate, moe_w_up, moe_w_down,
           ln_mix_g, ln_mix_b, ln_ffn_g, ln_ffn_b):
    nb = x.shape[0]
    meta = jnp.broadcast_to(meta_tokens.astype(x.dtype)[None], (nb, N_META, x.shape[-1]))
    h = jnp.concatenate([meta, x], axis=1)
    for i in range(DEPTH):
        j = i // N_MIXERS
        if i % N_MIXERS == 0:
            mix = short_conv_mixer(h, a_w_in[j], a_conv_w[j], a_w_out[j])
        else:
            mix = conformer_conv(h, b_w_pw1[j], b_b_pw1[j], b_conv_w[j], b_conv_b[j],
                                 b_ln_g[j], b_ln_b[j], b_w_pw2[j], b_b_pw2[j])
        h = layer_norm(DEEPNORM_ALPHA * h + mix, ln_mix_g[i], ln_mix_b[i])
        if i % 2 == 0:
            ff = swiglu(h, ffn_w_gate[j], ffn_w_up[j], ffn_w_down[j])
        else:
            ff = moe_swiglu(h, moe_router[j], moe_w_gate[j], moe_w_up[j], moe_w_down[j])
        h = layer_norm(DEEPNORM_ALPHA * h + ff, ln_ffn_g[i], ln_ffn_b[i])
    return h[:, N_META:]


def setup_inputs(seed: int = 0) -> dict:
    key = jax.random.key(seed)
    ks = jax.random.split(key, 24)
    D, F, FE, E = D_MODEL, D_FF, D_FF_EXPERT, N_EXPERTS
    nrm = lambda k, shape, scale: jax.random.normal(k, shape, jnp.float32) * scale
    gain = lambda k, shape: 1.0 + 0.02 * jax.random.normal(k, shape, jnp.float32)
    return {
        "x_prompt": nrm(ks[0], (BATCH, SEQ, D), 1.0),
        "x_sample": nrm(ks[1], (DEC_BATCH, DEC_SEQ, D), 1.0),
        "meta_tokens": nrm(ks[2], (N_META, D), 1.0),
        "a_w_in": nrm(ks[3], (N_EVEN, D, 3 * D), D ** -0.5),
        "a_conv_w": nrm(ks[4], (N_EVEN, SHORT_CONV_W, D), SHORT_CONV_W ** -0.5),
        "a_w_out": nrm(ks[5], (N_EVEN, D, D), D ** -0.5 * DEEPNORM_BETA),
        "b_w_pw1": nrm(ks[6], (N_ODD, D, 2 * D), D ** -0.5),
        "b_b_pw1": nrm(ks[7], (N_ODD, 2 * D), 0.02),
        "b_conv_w": nrm(ks[8], (N_ODD, CONFORMER_CONV_W, D), CONFORMER_CONV_W ** -0.5),
        "b_conv_b": nrm(ks[9], (N_ODD, D), 0.02),
        "b_ln_g": gain(ks[10], (N_ODD, D)),
        "b_ln_b": nrm(ks[11], (N_ODD, D), 0.02),
        "b_w_pw2": nrm(ks[12], (N_ODD, D, D), D ** -0.5 * DEEPNORM_BETA),
        "b_b_pw2": nrm(ks[13], (N_ODD, D), 0.02),
        "ffn_w_gate": nrm(ks[14], (N_EVEN, D, F), D ** -0.5),
        "ffn_w_up": nrm(ks[15], (N_EVEN, D, F), D ** -0.5),
        "ffn_w_down": nrm(ks[16], (N_EVEN, F, D), F ** -0.5 * DEEPNORM_BETA),
        "moe_router": nrm(ks[17], (N_ODD, D, E), D ** -0.5),
        "moe_w_gate": nrm(ks[18], (N_ODD, E, D, FE), D ** -0.5),
        "moe_w_up": nrm(ks[19], (N_ODD, E, D, FE), D ** -0.5),
        "moe_w_down": nrm(ks[20], (N_ODD, E, FE, D), FE ** -0.5 * DEEPNORM_BETA),
        "ln_mix_g": gain(ks[21], (DEPTH, D)),
        "ln_mix_b": nrm(ks[22], (DEPTH, D), 0.02),
        "ln_ffn_g": gain(ks[23], (DEPTH, D)),
        "ln_ffn_b": nrm(jax.random.fold_in(ks[23], 1), (DEPTH, D), 0.02),
    }


def reference(x_prompt, x_sample, meta_tokens,
              a_w_in, a_conv_w, a_w_out,
              b_w_pw1, b_b_pw1, b_conv_w, b_conv_b, b_ln_g, b_ln_b, b_w_pw2, b_b_pw2,
              ffn_w_gate, ffn_w_up, ffn_w_down,
              moe_router, moe_w_gate, moe_w_up, moe_w_down,
              ln_mix_g, ln_mix_b, ln_ffn_g, ln_ffn_b):
    weights = (meta_tokens,
               a_w_in, a_conv_w, a_w_out,
               b_w_pw1, b_b_pw1, b_conv_w, b_conv_b, b_ln_g, b_ln_b, b_w_pw2, b_b_pw2,
               ffn_w_gate, ffn_w_up, ffn_w_down,
               moe_router, moe_w_gate, moe_w_up, moe_w_down,
               ln_mix_g, ln_mix_b, ln_ffn_g, ln_ffn_b)
    y_prompt = encode(x_prompt, *weights)
    y_sample = encode(x_sample, *weights)
    return (y_prompt, y_sample)
```

```python
import functools

import jax
import jax.numpy as jnp
from jax import lax
from jax.experimental import pallas as pl
from jax.experimental.pallas import tpu as pltpu

N_META = 16
LN_EPS = 1e-5
HALO = 16
LANES = 128
BF16_ROWS = 16
MAX_ROW_TILE = 1024
MOE_ROW_TILE = 512
VMEM_LIMIT = 56 * 1024 * 1024

_BF16 = jnp.bfloat16
_F32 = jnp.float32


def _row_tile(rows):
    best = None
    for t in range(BF16_ROWS, MAX_ROW_TILE + 1, BF16_ROWS):
        if rows % t == 0:
            best = t
    assert best is not None, rows
    return best


def _chunks(total, size):
    out, s = [], 0
    while s < total:
        n = min(size, total - s)
        out.append((s, n))
        s += n
    return out


def _dot(a, b):
    return jnp.dot(a, b, preferred_element_type=_F32)


def _layer_norm(x, g, b):
    mu = jnp.mean(x, axis=-1, keepdims=True)
    xc = x - mu
    var = jnp.mean(xc * xc, axis=-1, keepdims=True)
    return xc * lax.rsqrt(var + LN_EPS) * g + b


def _silu(x):
    return x * jax.nn.sigmoid(x)


def _const_spec(shape):
    nd = len(shape)
    return pl.BlockSpec(shape, lambda *_: (0,) * nd, pipeline_mode=pl.Buffered(1))


def _seq_specs(t, d, n_halo_blocks):
    per_tile = t // HALO
    main = pl.BlockSpec((None, t, d), lambda b, i: (b, i, 0))
    prev = pl.BlockSpec((None, HALO, d), lambda b, i: (b, jnp.maximum(i * per_tile - 1, 0), 0))
    nxt = pl.BlockSpec((None, HALO, d),
                       lambda b, i: (b, jnp.minimum((i + 1) * per_tile, n_halo_blocks - 1), 0))
    return prev, main, nxt


def _fill_xbuf(xbuf, hp_ref, hm_ref, hn_ref, t):
    xbuf[0:HALO, :] = hp_ref[...].astype(_BF16)
    xbuf[HALO:HALO + t, :] = hm_ref[...].astype(_BF16)
    xbuf[HALO + t:HALO + t + HALO, :] = hn_ref[...].astype(_BF16)


def _halo_row_mask(t, d):
    i = pl.program_id(1)
    last = pl.num_programs(1) - 1
    r = lax.broadcasted_iota(jnp.int32, (t + 2 * HALO, d), 0)
    return jnp.logical_and(jnp.logical_or(r >= HALO, i > 0),
                           jnp.logical_or(r < HALO + t, i < last))


def _mixer_a_kernel(hp_ref, hm_ref, hn_ref, wb_ref, wc_ref, wh_ref, cw_ref, wo_ref,
                    g_ref, b_ref, out_ref, xbuf, pbuf, *, alpha):
    t, d = hm_ref.shape
    _fill_xbuf(xbuf, hp_ref, hm_ref, hn_ref, t)
    xa = xbuf[...]
    p = _dot(xa, wc_ref[...]) * _dot(xa, wh_ref[...])
    pbuf[...] = jnp.where(_halo_row_mask(t, d), p, 0.0)
    k = cw_ref.shape[0]
    conv = None
    for j in range(k):
        s = HALO - k // 2 + j
        term = cw_ref[j:j + 1, :] * pbuf[s:s + t, :]
        conv = term if conv is None else conv + term
    gate = _dot(xbuf[HALO:HALO + t, :], wb_ref[...])
    mix = _dot((gate * conv).astype(_BF16), wo_ref[...])
    out_ref[...] = _layer_norm(alpha * hm_ref[...] + mix, g_ref[...], b_ref[...])


def _mixer_a(h, w_in, conv_w, w_out, ln_g, ln_b, alpha):
    nb, s, d = h.shape
    t = _row_tile(s)
    prev, main, nxt = _seq_specs(t, d, s // HALO)
    k = conv_w.shape[0]
    w_spec = lambda j: pl.BlockSpec((d, d), lambda b, i, j=j: (0, j), pipeline_mode=pl.Buffered(1))
    return pl.pallas_call(
        functools.partial(_mixer_a_kernel, alpha=alpha),
        grid=(nb, s // t),
        in_specs=[prev, main, nxt, w_spec(0), w_spec(1), w_spec(2), _const_spec((k, d)),
                  _const_spec((d, d)), _const_spec((1, d)), _const_spec((1, d))],
        out_specs=pl.BlockSpec((None, t, d), lambda b, i: (b, i, 0)),
        out_shape=jax.ShapeDtypeStruct((nb, s, d), _F32),
        scratch_shapes=[pltpu.VMEM((t + 2 * HALO, d), _BF16), pltpu.VMEM((t + 2 * HALO, d), _F32)],
        compiler_params=pltpu.CompilerParams(
            dimension_semantics=("parallel", "parallel"), vmem_limit_bytes=VMEM_LIMIT),
        name="mixer_a",
    )(h, h, h, w_in, w_in, w_in, conv_w, w_out, ln_g, ln_b)


def _ffn_kernel(h_ref, wg_ref, wu_ref, wd_ref, g_ref, b_ref, out_ref, *, alpha, f_chunk):
    h = h_ref[...]
    x = h.astype(_BF16)
    acc = None
    for s, n in _chunks(wg_ref.shape[1], f_chunk):
        gate = _dot(x, wg_ref[:, s:s + n])
        up = _dot(x, wu_ref[:, s:s + n])
        part = _dot((_silu(gate) * up).astype(_BF16), wd_ref[s:s + n, :])
        acc = part if acc is None else acc + part
    out_ref[...] = _layer_norm(alpha * h + acc, g_ref[...], b_ref[...])


def _ffn(h, w_gate, w_up, w_down, ln_g, ln_b, alpha):
    nb, s, d = h.shape
    f = w_gate.shape[1]
    t = _row_tile(s)
    return pl.pallas_call(
        functools.partial(_ffn_kernel, alpha=alpha, f_chunk=1024),
        grid=(nb, s // t),
        in_specs=[pl.BlockSpec((None, t, d), lambda b, i: (b, i, 0)),
                  _const_spec((d, f)), _const_spec((d, f)), _const_spec((f, d)),
                  _const_spec((1, d)), _const_spec((1, d))],
        out_specs=pl.BlockSpec((None, t, d), lambda b, i: (b, i, 0)),
        out_shape=jax.ShapeDtypeStruct((nb, s, d), _F32),
        compiler_params=pltpu.CompilerParams(
            dimension_semantics=("parallel", "parallel"), vmem_limit_bytes=VMEM_LIMIT),
        name="ffn_dense",
    )(h, w_gate, w_up, w_down, ln_g, ln_b)


def _mixer_b_kernel(hp_ref, hm_ref, hn_ref, wa_ref, wg_ref, b1_ref, cw_ref, cb_ref, lg_ref, lb_ref,
                    w2_ref, b2_ref, g_ref, b_ref, rt_ref, out_ref, route_ref, xbuf, ubuf,
                    *, alpha, n_experts):
    t, d = hm_ref.shape
    _fill_xbuf(xbuf, hp_ref, hm_ref, hn_ref, t)
    xa = xbuf[...]
    a = _dot(xa, wa_ref[...]) + b1_ref[:, 0:d]
    gl = _dot(xa, wg_ref[...]) + b1_ref[:, d:2 * d]
    ubuf[...] = jnp.where(_halo_row_mask(t, d), a * jax.nn.sigmoid(gl), 0.0)
    k = cw_ref.shape[0]
    conv = None
    for j in range(k):
        s = HALO - k // 2 + j
        term = cw_ref[j:j + 1, :] * ubuf[s:s + t, :]
        conv = term if conv is None else conv + term
    u = _silu(_layer_norm(conv + cb_ref[...], lg_ref[...], lb_ref[...]))
    mix = _dot(u.astype(_BF16), w2_ref[...]) + b2_ref[...]
    h1 = _layer_norm(alpha * hm_ref[...] + mix, g_ref[...], b_ref[...])
    out_ref[...] = h1

    logits = jnp.dot(h1, rt_ref[...], preferred_element_type=_F32, precision=lax.Precision.HIGHEST)
    lane = lax.broadcasted_iota(jnp.int32, logits.shape, 1).astype(_F32)
    neg = jnp.float32(-jnp.inf)
    lg1 = jnp.where(lane < n_experts, logits, neg)
    m1 = jnp.max(lg1, axis=-1, keepdims=True)
    i1 = jnp.min(jnp.where(lg1 == m1, lane, float(LANES)), axis=-1, keepdims=True)
    lg2 = jnp.where(lane == i1, neg, lg1)
    m2 = jnp.max(lg2, axis=-1, keepdims=True)
    i2 = jnp.min(jnp.where(lg2 == m2, lane, float(LANES)), axis=-1, keepdims=True)
    e2 = jnp.exp(m2 - m1)
    den = 1.0 + e2
    route = jnp.where(lane == 0.0, i1, jnp.where(lane == 1.0, i2,
                      jnp.where(lane == 2.0, 1.0 / den, jnp.where(lane == 3.0, e2 / den, 0.0))))
    route_ref[...] = route


def _mixer_b(h, w_pw1, b_pw1, conv_w, conv_b, cln_g, cln_b, w_pw2, b_pw2, ln_g, ln_b, router_pad,
             alpha, n_experts):
    nb, s, d = h.shape
    t = _row_tile(s)
    prev, main, nxt = _seq_specs(t, d, s // HALO)
    k = conv_w.shape[0]
    w_spec = lambda j: pl.BlockSpec((d, d), lambda b, i, j=j: (0, j), pipeline_mode=pl.Buffered(1))
    vec = _const_spec((1, d))
    return pl.pallas_call(
        functools.partial(_mixer_b_kernel, alpha=alpha, n_experts=n_experts),
        grid=(nb, s // t),
        in_specs=[prev, main, nxt, w_spec(0), w_spec(1), _const_spec((1, 2 * d)),
                  _const_spec((k, d)), vec, vec, vec, _const_spec((d, d)), vec, vec, vec,
                  _const_spec((d, LANES))],
        out_specs=[pl.BlockSpec((None, t, d), lambda b, i: (b, i, 0)),
                   pl.BlockSpec((None, t, LANES), lambda b, i: (b, i, 0))],
        out_shape=[jax.ShapeDtypeStruct((nb, s, d), _F32),
                   jax.ShapeDtypeStruct((nb, s, LANES), _F32)],
        scratch_shapes=[pltpu.VMEM((t + 2 * HALO, d), _BF16), pltpu.VMEM((t + 2 * HALO, d), _F32)],
        compiler_params=pltpu.CompilerParams(
            dimension_semantics=("parallel", "parallel"), vmem_limit_bytes=VMEM_LIMIT),
        name="mixer_b",
    )(h, h, h, w_pw1, w_pw1, b_pw1, conv_w, conv_b, cln_g, cln_b, w_pw2, b_pw2, ln_g, ln_b,
      router_pad)


def _moe_kernel(texp_ref, nact_ref, src_ref, srcn_ref, dst_ref, x_hbm, wg_ref, wu_ref, wd_ref, y_hbm,
                xg, xb, acc, ob, gsem, ssem):
    del texp_ref
    m = pl.program_id(0)
    f = pl.program_id(1)
    nf = pl.num_programs(1)
    n_act = nact_ref[0]
    tm = xb.shape[0]
    slot = m % 2

    def gather_copy(idx_ref, r, s):
        return pltpu.make_async_copy(x_hbm.at[pl.ds(idx_ref[0, 0, r], 1), :],
                                     xg.at[s, pl.ds(r, 1), :], gsem.at[s])

    def scatter_copy(r, s):
        return pltpu.make_async_copy(ob.at[s, pl.ds(r, 1), :],
                                     y_hbm.at[pl.ds(dst_ref[0, 0, r], 1), :], ssem.at[s])

    def start_gather(idx_ref, s):
        def body(r, c):
            gather_copy(idx_ref, r, s).start()
            return c
        lax.fori_loop(0, tm, body, 0)

    def wait_rows(buf, sem, s):
        pltpu.make_async_copy(buf.at[s], buf.at[s], sem.at[s]).wait()

    active = m < n_act

    @pl.when(jnp.logical_and(active, f == 0))
    def _():
        @pl.when(m == 0)
        def _():
            start_gather(src_ref, 0)

        @pl.when(m + 1 < n_act)
        def _():
            start_gather(srcn_ref, 1 - slot)

        wait_rows(xg, gsem, slot)
        xb[...] = xg[slot].astype(_BF16)

    @pl.when(active)
    def _():
        x = xb[...]
        gate = _dot(x, wg_ref[...])
        up = _dot(x, wu_ref[...])
        part = _dot((_silu(gate) * up).astype(_BF16), wd_ref[...])

        @pl.when(f == 0)
        def _():
            acc[...] = part

        @pl.when(f > 0)
        def _():
            acc[...] += part

    @pl.when(jnp.logical_and(active, f == nf - 1))
    def _():
        @pl.when(m >= 2)
        def _():
            wait_rows(ob, ssem, slot)

        ob[slot] = acc[...]

        def body(r, c):
            scatter_copy(r, slot).start()
            return c
        lax.fori_loop(0, tm, body, 0)

        @pl.when(m == n_act - 1)
        def _():
            wait_rows(ob, ssem, slot)

            @pl.when(m >= 1)
            def _():
                wait_rows(ob, ssem, 1 - slot)


def _moe_ffn(x_rows, meta, w_gate, w_up, w_down, tm, f_chunk):
    n, d = x_rows.shape
    n_exp, _, fe = w_gate.shape
    tile_expert, n_act, src, dst = meta
    m_tiles = tile_expert.shape[0]
    nf = fe // f_chunk
    assert nf * f_chunk == fe
    src3 = src.reshape(m_tiles, 1, tm)
    dst3 = dst.reshape(m_tiles, 1, tm)
    smem_tile = lambda fn: pl.BlockSpec((1, 1, tm), fn, memory_space=pltpu.SMEM)
    chunk = lambda m, f, na: jnp.where(m < na[0], f, nf - 1)
    grid_spec = pltpu.PrefetchScalarGridSpec(
        num_scalar_prefetch=2,
        grid=(m_tiles, nf),
        in_specs=[
            smem_tile(lambda m, f, te, na: (m, 0, 0)),
            smem_tile(lambda m, f, te, na: (jnp.minimum(m + 1, m_tiles - 1), 0, 0)),
            smem_tile(lambda m, f, te, na: (m, 0, 0)),
            pl.BlockSpec(memory_space=pl.ANY),
            pl.BlockSpec((None, d, f_chunk), lambda m, f, te, na: (te[m], 0, chunk(m, f, na))),
            pl.BlockSpec((None, d, f_chunk), lambda m, f, te, na: (te[m], 0, chunk(m, f, na))),
            pl.BlockSpec((None, f_chunk, d), lambda m, f, te, na: (te[m], chunk(m, f, na), 0)),
        ],
        out_specs=pl.BlockSpec(memory_space=pl.ANY),
        scratch_shapes=[
            pltpu.VMEM((2, tm, d), _F32),
            pltpu.VMEM((tm, d), _BF16),
            pltpu.VMEM((tm, d), _F32),
            pltpu.VMEM((2, tm, d), _F32),
            pltpu.SemaphoreType.DMA((2,)),
            pltpu.SemaphoreType.DMA((2,)),
        ],
    )
    return pl.pallas_call(
        _moe_kernel,
        grid_spec=grid_spec,
        out_shape=jax.ShapeDtypeStruct((2 * n + m_tiles * tm, d), _F32),
        compiler_params=pltpu.CompilerParams(
            dimension_semantics=("arbitrary", "arbitrary"), vmem_limit_bytes=VMEM_LIMIT),
        name="moe_ffn",
    )(tile_expert, n_act, src3, src3, dst3, x_rows, w_gate, w_up, w_down)


def _route_metadata(route, n_experts, tm):
    n = route.shape[0]
    pairs = 2 * n
    m_tiles = pairs // tm + n_experts
    e = jnp.concatenate([route[:, 0], route[:, 1]]).astype(jnp.int32)
    onehot = (e[:, None] == jnp.arange(n_experts, dtype=jnp.int32)[None, :]).astype(jnp.int32)
    csum = jnp.cumsum(onehot, axis=0)
    rank = jnp.sum(csum * onehot, axis=1) - 1
    counts = csum[-1]
    padded = ((counts + tm - 1) // tm) * tm
    ends = jnp.cumsum(padded)
    offs = ends - padded
    pos = jnp.sum(onehot * offs[None, :], axis=1) + rank
    p = jnp.arange(pairs, dtype=jnp.int32)
    rows = m_tiles * tm
    src = jnp.zeros((rows,), jnp.int32).at[pos].set(p % n, unique_indices=True)
    dst = (pairs + jnp.arange(rows, dtype=jnp.int32)).at[pos].set(p, unique_indices=True)
    n_act = (ends[-1] // tm).astype(jnp.int32)
    tile_start = jnp.arange(m_tiles, dtype=jnp.int32) * tm
    tile_expert = jnp.sum((tile_start[:, None] >= ends[None, :]).astype(jnp.int32), axis=1)
    tile_expert = jnp.minimum(tile_expert, n_experts - 1)
    last_expert = jnp.sum((jnp.maximum(n_act - 1, 0) * tm >= ends).astype(jnp.int32))
    tile_expert = jnp.where(jnp.arange(m_tiles) < n_act, tile_expert,
                            jnp.minimum(last_expert, n_experts - 1)).astype(jnp.int32)
    return tile_expert, n_act.reshape(1), src, dst


def _combine_kernel(h_ref, y0_ref, y1_ref, route_ref, g_ref, b_ref, out_ref, *, alpha):
    w0 = route_ref[:, 2:3]
    w1 = route_ref[:, 3:4]
    ff = w0 * y0_ref[...] + w1 * y1_ref[...]
    out_ref[...] = _layer_norm(alpha * h_ref[...] + ff, g_ref[...], b_ref[...])


def _combine(h_rows, y, route, ln_g, ln_b, alpha, t):
    n, d = h_rows.shape
    nt = n // t
    return pl.pallas_call(
        functools.partial(_combine_kernel, alpha=alpha),
        grid=(nt,),
        in_specs=[pl.BlockSpec((t, d), lambda i: (i, 0)),
                  pl.BlockSpec((t, d), lambda i: (i, 0)),
                  pl.BlockSpec((t, d), lambda i: (i + nt, 0)),
                  pl.BlockSpec((t, LANES), lambda i: (i, 0)),
                  _const_spec((1, d)), _const_spec((1, d))],
        out_specs=pl.BlockSpec((t, d), lambda i: (i, 0)),
        out_shape=jax.ShapeDtypeStruct((n, d), _F32),
        compiler_params=pltpu.CompilerParams(
            dimension_semantics=("parallel",), vmem_limit_bytes=VMEM_LIMIT),
        name="moe_combine",
    )(h_rows, y, y, route, ln_g, ln_b)


def _encode(x, p, moe_tile):
    nb, seq, d = x.shape
    depth = p["ln_mix_g"].shape[0]
    alpha = (2.0 * depth) ** 0.25
    n_experts = p["moe_router"].shape[-1]
    meta = jnp.broadcast_to(p["meta_tokens"].astype(x.dtype)[None], (nb, N_META, d))
    h = jnp.concatenate([meta, x], axis=1)
    s = h.shape[1]
    row = lambda v: v.reshape(1, -1)
    for i in range(depth):
        j = i // 2
        if i % 2 == 0:
            h = _mixer_a(h, p["a_w_in"][j], p["a_conv_w"][j], p["a_w_out"][j],
                         row(p["ln_mix_g"][i]), row(p["ln_mix_b"][i]), alpha)
            h = _ffn(h, p["ffn_w_gate"][j], p["ffn_w_up"][j], p["ffn_w_down"][j],
                     row(p["ln_ffn_g"][i]), row(p["ln_ffn_b"][i]), alpha)
        else:
            router_pad = jnp.pad(p["moe_router"][j].astype(_F32), ((0, 0), (0, LANES - n_experts)))
            h1, route = _mixer_b(h, p["b_w_pw1"][j], row(p["b_b_pw1"][j]), p["b_conv_w"][j],
                                 row(p["b_conv_b"][j]), row(p["b_ln_g"][j]), row(p["b_ln_b"][j]),
                                 p["b_w_pw2"][j], row(p["b_b_pw2"][j]),
                                 row(p["ln_mix_g"][i]), row(p["ln_mix_b"][i]), router_pad,
                                 alpha, n_experts)
            h1r = h1.reshape(nb * s, d)
            router = route.reshape(nb * s, LANES)
            plan = _route_metadata(router, n_experts, moe_tile)
            fe = p["moe_w_gate"].shape[-1]
            y = _moe_ffn(h1r, plan, p["moe_w_gate"][j], p["moe_w_up"][j], p["moe_w_down"][j],
                         moe_tile, fe // 2)
            h = _combine(h1r, y, router, row(p["ln_ffn_g"][i]), row(p["ln_ffn_b"][i]), alpha,
                         _row_tile(s)).reshape(nb, s, d)
    return h[:, N_META:]


_MATMUL_WEIGHTS = ("a_w_in", "a_w_out", "b_w_pw1", "b_w_pw2", "ffn_w_gate", "ffn_w_up", "ffn_w_down",
                   "moe_w_gate", "moe_w_up", "moe_w_down")


def kernel(x_prompt, x_sample, meta_tokens, a_w_in, a_conv_w, a_w_out, b_w_pw1, b_b_pw1, b_conv_w,
           b_conv_b, b_ln_g, b_ln_b, b_w_pw2, b_b_pw2, ffn_w_gate, ffn_w_up, ffn_w_down, moe_router,
           moe_w_gate, moe_w_up, moe_w_down, ln_mix_g, ln_mix_b, ln_ffn_g, ln_ffn_b):
    p = dict(meta_tokens=meta_tokens, a_w_in=a_w_in, a_conv_w=a_conv_w, a_w_out=a_w_out,
             b_w_pw1=b_w_pw1, b_b_pw1=b_b_pw1, b_conv_w=b_conv_w, b_conv_b=b_conv_b, b_ln_g=b_ln_g,
             b_ln_b=b_ln_b, b_w_pw2=b_w_pw2, b_b_pw2=b_b_pw2, ffn_w_gate=ffn_w_gate,
             ffn_w_up=ffn_w_up, ffn_w_down=ffn_w_down, moe_router=moe_router, moe_w_gate=moe_w_gate,
             moe_w_up=moe_w_up, moe_w_down=moe_w_down, ln_mix_g=ln_mix_g, ln_mix_b=ln_mix_b,
             ln_ffn_g=ln_ffn_g, ln_ffn_b=ln_ffn_b)
    for name in _MATMUL_WEIGHTS:
        p[name] = p[name].astype(_BF16)
    return (_encode(x_prompt, p, MOE_ROW_TILE), _encode(x_sample, p, MOE_ROW_TILE))
```

```python
import functools

import jax
import jax.numpy as jnp
from jax import lax
from jax.experimental import pallas as pl
from jax.experimental.pallas import tpu as pltpu

N_META = 16
LN_EPS = 1e-5
HALO = 16
LANES = 128
BF16_ROWS = 16
MAX_ROW_TILE = 1024
MOE_ROW_TILE = 512
CONV_SLAB_GROUP = 4
VMEM_LIMIT = 56 * 1024 * 1024

_BF16 = jnp.bfloat16
_F32 = jnp.float32


def _row_tile(rows):
    best = None
    for t in range(BF16_ROWS, MAX_ROW_TILE + 1, BF16_ROWS):
        if rows % t == 0:
            best = t
    assert best is not None, rows
    return best


def _chunks(total, size):
    out, s = [], 0
    while s < total:
        n = min(size, total - s)
        out.append((s, n))
        s += n
    return out


def _dot(a, b):
    return jnp.dot(a, b, preferred_element_type=_F32)


def _layer_norm(x, g, b):
    mu = jnp.mean(x, axis=-1, keepdims=True)
    xc = x - mu
    var = jnp.mean(xc * xc, axis=-1, keepdims=True)
    return xc * lax.rsqrt(var + LN_EPS) * g + b


def _silu(x):
    return x * jax.nn.sigmoid(x)


def _const_spec(shape):
    nd = len(shape)
    return pl.BlockSpec(shape, lambda *_: (0,) * nd, pipeline_mode=pl.Buffered(1))


def _seq_specs(t, d, n_halo_blocks):
    per_tile = t // HALO
    main = pl.BlockSpec((None, t, d), lambda b, i: (b, i, 0))
    prev = pl.BlockSpec((None, HALO, d), lambda b, i: (b, jnp.maximum(i * per_tile - 1, 0), 0))
    nxt = pl.BlockSpec((None, HALO, d),
                       lambda b, i: (b, jnp.minimum((i + 1) * per_tile, n_halo_blocks - 1), 0))
    return prev, main, nxt


def _fill_xbuf(xbuf, hp_ref, hm_ref, hn_ref, t):
    xbuf[0:HALO, :] = hp_ref[...].astype(_BF16)
    xbuf[HALO:HALO + t, :] = hm_ref[...].astype(_BF16)
    xbuf[HALO + t:HALO + t + HALO, :] = hn_ref[...].astype(_BF16)


def _halo_row_mask(t, d):
    i = pl.program_id(1)
    last = pl.num_programs(1) - 1
    r = lax.broadcasted_iota(jnp.int32, (t + 2 * HALO, d), 0)
    return jnp.logical_and(jnp.logical_or(r >= HALO, i > 0),
                           jnp.logical_or(r < HALO + t, i < last))


def _mixer_a_kernel(hp_ref, hm_ref, hn_ref, wb_ref, wc_ref, wh_ref, cw_ref, wo_ref,
                    g_ref, b_ref, out_ref, xbuf, pbuf, *, alpha):
    t, d = hm_ref.shape
    _fill_xbuf(xbuf, hp_ref, hm_ref, hn_ref, t)
    xa = xbuf[...]
    p = _dot(xa, wc_ref[...]) * _dot(xa, wh_ref[...])
    pbuf[...] = jnp.where(_halo_row_mask(t, d), p, 0.0)
    k = cw_ref.shape[0]
    conv = None
    for j in range(k):
        s = HALO - k // 2 + j
        term = cw_ref[j:j + 1, :] * pbuf[s:s + t, :]
        conv = term if conv is None else conv + term
    gate = _dot(xbuf[HALO:HALO + t, :], wb_ref[...])
    mix = _dot((gate * conv).astype(_BF16), wo_ref[...])
    out_ref[...] = _layer_norm(alpha * hm_ref[...] + mix, g_ref[...], b_ref[...])


def _mixer_a(h, w_in, conv_w, w_out, ln_g, ln_b, alpha):
    nb, s, d = h.shape
    t = _row_tile(s)
    prev, main, nxt = _seq_specs(t, d, s // HALO)
    k = conv_w.shape[0]
    w_spec = lambda j: pl.BlockSpec((d, d), lambda b, i, j=j: (0, j), pipeline_mode=pl.Buffered(1))
    return pl.pallas_call(
        functools.partial(_mixer_a_kernel, alpha=alpha),
        grid=(nb, s // t),
        in_specs=[prev, main, nxt, w_spec(0), w_spec(1), w_spec(2), _const_spec((k, d)),
                  _const_spec((d, d)), _const_spec((1, d)), _const_spec((1, d))],
        out_specs=pl.BlockSpec((None, t, d), lambda b, i: (b, i, 0)),
        out_shape=jax.ShapeDtypeStruct((nb, s, d), _F32),
        scratch_shapes=[pltpu.VMEM((t + 2 * HALO, d), _BF16), pltpu.VMEM((t + 2 * HALO, d), _F32)],
        compiler_params=pltpu.CompilerParams(
            dimension_semantics=("parallel", "parallel"), vmem_limit_bytes=VMEM_LIMIT),
        name="mixer_a",
    )(h, h, h, w_in, w_in, w_in, conv_w, w_out, ln_g, ln_b)


def _ffn_kernel(h_ref, wg_ref, wu_ref, wd_ref, g_ref, b_ref, out_ref, *, alpha, f_chunk):
    h = h_ref[...]
    x = h.astype(_BF16)
    acc = None
    for s, n in _chunks(wg_ref.shape[1], f_chunk):
        gate = _dot(x, wg_ref[:, s:s + n])
        up = _dot(x, wu_ref[:, s:s + n])
        part = _dot((_silu(gate) * up).astype(_BF16), wd_ref[s:s + n, :])
        acc = part if acc is None else acc + part
    out_ref[...] = _layer_norm(alpha * h + acc, g_ref[...], b_ref[...])


def _ffn(h, w_gate, w_up, w_down, ln_g, ln_b, alpha):
    nb, s, d = h.shape
    f = w_gate.shape[1]
    t = _row_tile(s)
    return pl.pallas_call(
        functools.partial(_ffn_kernel, alpha=alpha, f_chunk=1024),
        grid=(nb, s // t),
        in_specs=[pl.BlockSpec((None, t, d), lambda b, i: (b, i, 0)),
                  _const_spec((d, f)), _const_spec((d, f)), _const_spec((f, d)),
                  _const_spec((1, d)), _const_spec((1, d))],
        out_specs=pl.BlockSpec((None, t, d), lambda b, i: (b, i, 0)),
        out_shape=jax.ShapeDtypeStruct((nb, s, d), _F32),
        compiler_params=pltpu.CompilerParams(
            dimension_semantics=("parallel", "parallel"), vmem_limit_bytes=VMEM_LIMIT),
        name="ffn_dense",
    )(h, w_gate, w_up, w_down, ln_g, ln_b)


def _depthwise_conv(ubuf, wbc, cbuf, t, d, k):
    off0 = HALO - k // 2
    taps = {}
    for kk in range(k):
        a, b = divmod(off0 + kk, 8)
        taps.setdefault(b, []).append((a, kk))
    bs = sorted(taps)
    a_all = sorted({a for lst in taps.values() for a, _ in lst})
    assert t % 8 == 0 and (t + 2 * HALO) // 8 >= t // 8 + a_all[-1] + 1
    n_slab = t // 8
    group = min(CONV_SLAB_GROUP, n_slab)
    rows = lax.broadcasted_iota(jnp.int32, (8, LANES), 0)

    def slab_start(m):
        return m * 8 if isinstance(m, int) else pl.multiple_of(m * 8, 8)

    def tree_sum(vals):
        while len(vals) > 1:
            vals = [vals[i] + vals[i + 1] for i in range(0, len(vals) - 1, 2)] + vals[len(vals) & ~1:]
        return vals[0]

    for c in range(d // LANES):
        lanes = pl.ds(c * LANES, LANES)

        def zslab(m, lanes=lanes):
            u = {a: ubuf[pl.ds(slab_start(m + a), 8), lanes] for a in a_all}
            return tuple(tree_sum([wbc[kk * 8:(kk + 1) * 8, lanes] * u[a] for a, kk in taps[b]])
                         for b in bs)

        def combine(zp, zn):
            return tree_sum([p if b == 0 else pltpu.roll(jnp.where(rows < b, n, p), 8 - b, 0)
                             for b, p, n in zip(bs, zp, zn)])

        def slabs(j0, count, zp, lanes=lanes):
            for g in range(count):
                zn = zslab(j0 + g + 1)
                cbuf[pl.ds(slab_start(j0 + g), 8), lanes] = combine(zp, zn)
                zp = zn
            return zp

        n_loop = n_slab // group
        zp = lax.fori_loop(0, n_loop, lambda i, z: slabs(i * group, group, z), zslab(0))
        slabs(n_loop * group, n_slab - n_loop * group, zp)


def _mixer_b_kernel(hp_ref, hm_ref, hn_ref, wa_ref, wg_ref, b1_ref, cw_ref, cb_ref, lg_ref, lb_ref,
                    w2_ref, b2_ref, g_ref, b_ref, rt_ref, out_ref, route_ref, xbuf, ubuf, cbuf, wbc,
                    *, alpha, n_experts):
    t, d = hm_ref.shape
    _fill_xbuf(xbuf, hp_ref, hm_ref, hn_ref, t)
    xa = xbuf[...]
    a = _dot(xa, wa_ref[...]) + b1_ref[:, 0:d]
    gl = _dot(xa, wg_ref[...]) + b1_ref[:, d:2 * d]
    ubuf[...] = jnp.where(_halo_row_mask(t, d), a * jax.nn.sigmoid(gl), 0.0)
    k = cw_ref.shape[0]
    for kk in range(k):
        wbc[kk * 8:(kk + 1) * 8, :] = jnp.broadcast_to(cw_ref[kk:kk + 1, :], (8, d))
    _depthwise_conv(ubuf, wbc, cbuf, t, d, k)
    u = _silu(_layer_norm(cbuf[...] + cb_ref[...], lg_ref[...], lb_ref[...]))
    mix = _dot(u.astype(_BF16), w2_ref[...]) + b2_ref[...]
    h1 = _layer_norm(alpha * hm_ref[...] + mix, g_ref[...], b_ref[...])
    out_ref[...] = h1

    hi = h1.astype(_BF16)
    lo = (h1 - hi.astype(_F32)).astype(_BF16)
    p_hi = _dot(hi, rt_ref[...])
    logits = p_hi[:, 0:LANES] + p_hi[:, LANES:2 * LANES] + _dot(lo, rt_ref[:, 0:LANES])
    lane = lax.broadcasted_iota(jnp.int32, logits.shape, 1).astype(_F32)
    neg = jnp.float32(-jnp.inf)
    lg1 = jnp.where(lane < n_experts, logits, neg)
    m1 = jnp.max(lg1, axis=-1, keepdims=True)
    i1 = jnp.min(jnp.where(lg1 == m1, lane, float(LANES)), axis=-1, keepdims=True)
    lg2 = jnp.where(lane == i1, neg, lg1)
    m2 = jnp.max(lg2, axis=-1, keepdims=True)
    i2 = jnp.min(jnp.where(lg2 == m2, lane, float(LANES)), axis=-1, keepdims=True)
    e2 = jnp.exp(m2 - m1)
    den = 1.0 + e2
    route = jnp.where(lane == 0.0, i1, jnp.where(lane == 1.0, i2,
                      jnp.where(lane == 2.0, 1.0 / den, jnp.where(lane == 3.0, e2 / den, 0.0))))
    route_ref[...] = route


def _mixer_b(h, w_pw1, b_pw1, conv_w, conv_b, cln_g, cln_b, w_pw2, b_pw2, ln_g, ln_b, router_pad,
             alpha, n_experts):
    nb, s, d = h.shape
    t = _row_tile(s)
    prev, main, nxt = _seq_specs(t, d, s // HALO)
    k = conv_w.shape[0]
    w_spec = lambda j: pl.BlockSpec((d, d), lambda b, i, j=j: (0, j), pipeline_mode=pl.Buffered(1))
    vec = _const_spec((1, d))
    return pl.pallas_call(
        functools.partial(_mixer_b_kernel, alpha=alpha, n_experts=n_experts),
        grid=(nb, s // t),
        in_specs=[prev, main, nxt, w_spec(0), w_spec(1), _const_spec((1, 2 * d)),
                  _const_spec((k, d)), vec, vec, vec, _const_spec((d, d)), vec, vec, vec,
                  _const_spec((d, 2 * LANES))],
        out_specs=[pl.BlockSpec((None, t, d), lambda b, i: (b, i, 0)),
                   pl.BlockSpec((None, t, LANES), lambda b, i: (b, i, 0))],
        out_shape=[jax.ShapeDtypeStruct((nb, s, d), _F32),
                   jax.ShapeDtypeStruct((nb, s, LANES), _F32)],
        scratch_shapes=[pltpu.VMEM((t + 2 * HALO, d), _BF16), pltpu.VMEM((t + 2 * HALO, d), _F32),
                        pltpu.VMEM((t, d), _F32), pltpu.VMEM((8 * k, d), _F32)],
        compiler_params=pltpu.CompilerParams(
            dimension_semantics=("parallel", "parallel"), vmem_limit_bytes=VMEM_LIMIT),
        name="mixer_b",
    )(h, h, h, w_pw1, w_pw1, b_pw1, conv_w, conv_b, cln_g, cln_b, w_pw2, b_pw2, ln_g, ln_b,
      router_pad)


def _moe_kernel(texp_ref, nact_ref, src_ref, srcn_ref, dst_ref, x_hbm, wg_ref, wu_ref, wd_ref, y_hbm,
                xg, xb, acc, ob, gsem, ssem):
    del texp_ref
    m = pl.program_id(0)
    f = pl.program_id(1)
    nf = pl.num_programs(1)
    n_act = nact_ref[0]
    tm = xb.shape[0]
    slot = m % 2

    def gather_copy(idx_ref, r, s):
        return pltpu.make_async_copy(x_hbm.at[pl.ds(idx_ref[0, 0, r], 1), :],
                                     xg.at[s, pl.ds(r, 1), :], gsem.at[s])

    def scatter_copy(r, s):
        return pltpu.make_async_copy(ob.at[s, pl.ds(r, 1), :],
                                     y_hbm.at[pl.ds(dst_ref[0, 0, r], 1), :], ssem.at[s])

    def start_gather(idx_ref, s):
        def body(r, c):
            gather_copy(idx_ref, r, s).start()
            return c
        lax.fori_loop(0, tm, body, 0)

    def wait_rows(buf, sem, s):
        pltpu.make_async_copy(buf.at[s], buf.at[s], sem.at[s]).wait()

    active = m < n_act

    @pl.when(jnp.logical_and(active, f == 0))
    def _():
        @pl.when(m == 0)
        def _():
            start_gather(src_ref, 0)

        @pl.when(m + 1 < n_act)
        def _():
            start_gather(srcn_ref, 1 - slot)

        wait_rows(xg, gsem, slot)
        xb[...] = xg[slot].astype(_BF16)

    @pl.when(active)
    def _():
        x = xb[...]
        gate = _dot(x, wg_ref[...])
        up = _dot(x, wu_ref[...])
        part = _dot((_silu(gate) * up).astype(_BF16), wd_ref[...])

        @pl.when(f == 0)
        def _():
            acc[...] = part

        @pl.when(f > 0)
        def _():
            acc[...] += part

    @pl.when(jnp.logical_and(active, f == nf - 1))
    def _():
        @pl.when(m >= 2)
        def _():
            wait_rows(ob, ssem, slot)

        ob[slot] = acc[...]

        def body(r, c):
            scatter_copy(r, slot).start()
            return c
        lax.fori_loop(0, tm, body, 0)

        @pl.when(m == n_act - 1)
        def _():
            wait_rows(ob, ssem, slot)

            @pl.when(m >= 1)
            def _():
                wait_rows(ob, ssem, 1 - slot)


def _moe_ffn(x_rows, meta, w_gate, w_up, w_down, tm, f_chunk):
    n, d = x_rows.shape
    n_exp, _, fe = w_gate.shape
    tile_expert, n_act, src, dst = meta
    m_tiles = tile_expert.shape[0]
    nf = fe // f_chunk
    assert nf * f_chunk == fe
    src3 = src.reshape(m_tiles, 1, tm)
    dst3 = dst.reshape(m_tiles, 1, tm)
    smem_tile = lambda fn: pl.BlockSpec((1, 1, tm), fn, memory_space=pltpu.SMEM)
    chunk = lambda m, f, na: jnp.where(m < na[0], f, nf - 1)
    grid_spec = pltpu.PrefetchScalarGridSpec(
        num_scalar_prefetch=2,
        grid=(m_tiles, nf),
        in_specs=[
            smem_tile(lambda m, f, te, na: (m, 0, 0)),
            smem_tile(lambda m, f, te, na: (jnp.minimum(m + 1, m_tiles - 1), 0, 0)),
            smem_tile(lambda m, f, te, na: (m, 0, 0)),
            pl.BlockSpec(memory_space=pl.ANY),
            pl.BlockSpec((None, d, f_chunk), lambda m, f, te, na: (te[m], 0, chunk(m, f, na))),
            pl.BlockSpec((None, d, f_chunk), lambda m, f, te, na: (te[m], 0, chunk(m, f, na))),
            pl.BlockSpec((None, f_chunk, d), lambda m, f, te, na: (te[m], chunk(m, f, na), 0)),
        ],
        out_specs=pl.BlockSpec(memory_space=pl.ANY),
        scratch_shapes=[
            pltpu.VMEM((2, tm, d), _F32),
            pltpu.VMEM((tm, d), _BF16),
            pltpu.VMEM((tm, d), _F32),
            pltpu.VMEM((2, tm, d), _F32),
            pltpu.SemaphoreType.DMA((2,)),
            pltpu.SemaphoreType.DMA((2,)),
        ],
    )
    return pl.pallas_call(
        _moe_kernel,
        grid_spec=grid_spec,
        out_shape=jax.ShapeDtypeStruct((2 * n + m_tiles * tm, d), _F32),
        compiler_params=pltpu.CompilerParams(
            dimension_semantics=("arbitrary", "arbitrary"), vmem_limit_bytes=VMEM_LIMIT),
        name="moe_ffn",
    )(tile_expert, n_act, src3, src3, dst3, x_rows, w_gate, w_up, w_down)


def _route_metadata(route, n_experts, tm):
    n = route.shape[0]
    pairs = 2 * n
    m_tiles = pairs // tm + n_experts
    e = jnp.concatenate([route[:, 0], route[:, 1]]).astype(jnp.int32)
    onehot = (e[:, None] == jnp.arange(n_experts, dtype=jnp.int32)[None, :]).astype(jnp.int32)
    csum = jnp.cumsum(onehot, axis=0)
    rank = jnp.sum(csum * onehot, axis=1) - 1
    counts = csum[-1]
    padded = ((counts + tm - 1) // tm) * tm
    ends = jnp.cumsum(padded)
    offs = ends - padded
    pos = jnp.sum(onehot * offs[None, :], axis=1) + rank
    p = jnp.arange(pairs, dtype=jnp.int32)
    rows = m_tiles * tm
    src = jnp.zeros((rows,), jnp.int32).at[pos].set(p % n, unique_indices=True)
    dst = (pairs + jnp.arange(rows, dtype=jnp.int32)).at[pos].set(p, unique_indices=True)
    n_act = (ends[-1] // tm).astype(jnp.int32)
    tile_start = jnp.arange(m_tiles, dtype=jnp.int32) * tm
    tile_expert = jnp.sum((tile_start[:, None] >= ends[None, :]).astype(jnp.int32), axis=1)
    tile_expert = jnp.minimum(tile_expert, n_experts - 1)
    last_expert = jnp.sum((jnp.maximum(n_act - 1, 0) * tm >= ends).astype(jnp.int32))
    tile_expert = jnp.where(jnp.arange(m_tiles) < n_act, tile_expert,
                            jnp.minimum(last_expert, n_experts - 1)).astype(jnp.int32)
    return tile_expert, n_act.reshape(1), src, dst


def _combine_kernel(h_ref, y0_ref, y1_ref, route_ref, g_ref, b_ref, out_ref, *, alpha):
    w0 = route_ref[:, 2:3]
    w1 = route_ref[:, 3:4]
    ff = w0 * y0_ref[...] + w1 * y1_ref[...]
    out_ref[...] = _layer_norm(alpha * h_ref[...] + ff, g_ref[...], b_ref[...])


def _combine(h_rows, y, route, ln_g, ln_b, alpha, t):
    n, d = h_rows.shape
    nt = n // t
    return pl.pallas_call(
        functools.partial(_combine_kernel, alpha=alpha),
        grid=(nt,),
        in_specs=[pl.BlockSpec((t, d), lambda i: (i, 0)),
                  pl.BlockSpec((t, d), lambda i: (i, 0)),
                  pl.BlockSpec((t, d), lambda i: (i + nt, 0)),
                  pl.BlockSpec((t, LANES), lambda i: (i, 0)),
                  _const_spec((1, d)), _const_spec((1, d))],
        out_specs=pl.BlockSpec((t, d), lambda i: (i, 0)),
        out_shape=jax.ShapeDtypeStruct((n, d), _F32),
        compiler_params=pltpu.CompilerParams(
            dimension_semantics=("parallel",), vmem_limit_bytes=VMEM_LIMIT),
        name="moe_combine",
    )(h_rows, y, y, route, ln_g, ln_b)


def _encode(x, p, moe_tile):
    nb, seq, d = x.shape
    depth = p["ln_mix_g"].shape[0]
    alpha = (2.0 * depth) ** 0.25
    n_experts = p["moe_router"].shape[-1]
    meta = jnp.broadcast_to(p["meta_tokens"].astype(x.dtype)[None], (nb, N_META, d))
    h = jnp.concatenate([meta, x], axis=1)
    s = h.shape[1]
    row = lambda v: v.reshape(1, -1)
    for i in range(depth):
        j = i // 2
        if i % 2 == 0:
            h = _mixer_a(h, p["a_w_in"][j], p["a_conv_w"][j], p["a_w_out"][j],
                         row(p["ln_mix_g"][i]), row(p["ln_mix_b"][i]), alpha)
            h = _ffn(h, p["ffn_w_gate"][j], p["ffn_w_up"][j], p["ffn_w_down"][j],
                     row(p["ln_ffn_g"][i]), row(p["ln_ffn_b"][i]), alpha)
        else:
            router_f32 = jnp.pad(p["moe_router"][j].astype(_F32), ((0, 0), (0, LANES - n_experts)))
            router_hi = router_f32.astype(_BF16)
            router_lo = (router_f32 - router_hi.astype(_F32)).astype(_BF16)
            router_pad = jnp.concatenate([router_hi, router_lo], axis=1)
            h1, route = _mixer_b(h, p["b_w_pw1"][j], row(p["b_b_pw1"][j]), p["b_conv_w"][j],
                                 row(p["b_conv_b"][j]), row(p["b_ln_g"][j]), row(p["b_ln_b"][j]),
                                 p["b_w_pw2"][j], row(p["b_b_pw2"][j]),
                                 row(p["ln_mix_g"][i]), row(p["ln_mix_b"][i]), router_pad,
                                 alpha, n_experts)
            h1r = h1.reshape(nb * s, d)
            router = route.reshape(nb * s, LANES)
            plan = _route_metadata(router, n_experts, moe_tile)
            fe = p["moe_w_gate"].shape[-1]
            y = _moe_ffn(h1r, plan, p["moe_w_gate"][j], p["moe_w_up"][j], p["moe_w_down"][j],
                         moe_tile, fe // 2)
            h = _combine(h1r, y, router, row(p["ln_ffn_g"][i]), row(p["ln_ffn_b"][i]), alpha,
                         _row_tile(s)).reshape(nb, s, d)
    return h[:, N_META:]


_MATMUL_WEIGHTS = ("a_w_in", "a_w_out", "b_w_pw1", "b_w_pw2", "ffn_w_gate", "ffn_w_up", "ffn_w_down",
                   "moe_w_gate", "moe_w_up", "moe_w_down")


def kernel(x_prompt, x_sample, meta_tokens, a_w_in, a_conv_w, a_w_out, b_w_pw1, b_b_pw1, b_conv_w,
           b_conv_b, b_ln_g, b_ln_b, b_w_pw2, b_b_pw2, ffn_w_gate, ffn_w_up, ffn_w_down, moe_router,
           moe_w_gate, moe_w_up, moe_w_down, ln_mix_g, ln_mix_b, ln_ffn_g, ln_ffn_b):
    p = dict(meta_tokens=meta_tokens, a_w_in=a_w_in, a_conv_w=a_conv_w, a_w_out=a_w_out,
             b_w_pw1=b_w_pw1, b_b_pw1=b_b_pw1, b_conv_w=b_conv_w, b_conv_b=b_conv_b, b_ln_g=b_ln_g,
             b_ln_b=b_ln_b, b_w_pw2=b_w_pw2, b_b_pw2=b_b_pw2, ffn_w_gate=ffn_w_gate,
             ffn_w_up=ffn_w_up, ffn_w_down=ffn_w_down, moe_router=moe_router, moe_w_gate=moe_w_gate,
             moe_w_up=moe_w_up, moe_w_down=moe_w_down, ln_mix_g=ln_mix_g, ln_mix_b=ln_mix_b,
             ln_ffn_g=ln_ffn_g, ln_ffn_b=ln_ffn_b)
    for name in _MATMUL_WEIGHTS:
        p[name] = p[name].astype(_BF16)
    return (_encode(x_prompt, p, MOE_ROW_TILE), _encode(x_sample, p, MOE_ROW_TILE))
```

```python
import functools

import jax
import jax.numpy as jnp
from jax import lax
from jax.experimental import pallas as pl
from jax.experimental.pallas import tpu as pltpu

N_META = 16
LN_EPS = 1e-5
HALO = 16
LANES = 128
BF16_ROWS = 16
MAX_ROW_TILE = 1024
MOE_ROW_TILE = 512
CONV_SLAB_GROUP = 4
VMEM_LIMIT = 56 * 1024 * 1024

_BF16 = jnp.bfloat16
_F32 = jnp.float32


def _row_tile(rows):
    best = None
    for t in range(BF16_ROWS, MAX_ROW_TILE + 1, BF16_ROWS):
        if rows % t == 0:
            best = t
    assert best is not None, rows
    return best


def _chunks(total, size):
    out, s = [], 0
    while s < total:
        n = min(size, total - s)
        out.append((s, n))
        s += n
    return out


def _dot(a, b):
    return jnp.dot(a, b, preferred_element_type=_F32)


def _layer_norm(x, g, b):
    mu = jnp.mean(x, axis=-1, keepdims=True)
    xc = x - mu
    var = jnp.mean(xc * xc, axis=-1, keepdims=True)
    return xc * lax.rsqrt(var + LN_EPS) * g + b


def _silu(x):
    return x * jax.nn.sigmoid(x)


def _const_spec(shape):
    nd = len(shape)
    return pl.BlockSpec(shape, lambda *_: (0,) * nd, pipeline_mode=pl.Buffered(1))


def _seq_specs(t, d, n_halo_blocks):
    per_tile = t // HALO
    main = pl.BlockSpec((None, t, d), lambda b, i: (b, i, 0))
    prev = pl.BlockSpec((None, HALO, d), lambda b, i: (b, jnp.maximum(i * per_tile - 1, 0), 0))
    nxt = pl.BlockSpec((None, HALO, d),
                       lambda b, i: (b, jnp.minimum((i + 1) * per_tile, n_halo_blocks - 1), 0))
    return prev, main, nxt


def _fill_xbuf(xbuf, hp_ref, hm_ref, hn_ref, t):
    xbuf[0:HALO, :] = hp_ref[...].astype(_BF16)
    xbuf[HALO:HALO + t, :] = hm_ref[...].astype(_BF16)
    xbuf[HALO + t:HALO + t + HALO, :] = hn_ref[...].astype(_BF16)


def _halo_row_mask(t, d):
    i = pl.program_id(1)
    last = pl.num_programs(1) - 1
    r = lax.broadcasted_iota(jnp.int32, (t + 2 * HALO, d), 0)
    return jnp.logical_and(jnp.logical_or(r >= HALO, i > 0),
                           jnp.logical_or(r < HALO + t, i < last))


def _mixer_a_kernel(hp_ref, hm_ref, hn_ref, wb_ref, wc_ref, wh_ref, cw_ref, wo_ref,
                    g_ref, b_ref, out_ref, xbuf, pbuf, *, alpha):
    t, d = hm_ref.shape
    _fill_xbuf(xbuf, hp_ref, hm_ref, hn_ref, t)
    xa = xbuf[...]
    p = _dot(xa, wc_ref[...]) * _dot(xa, wh_ref[...])
    pbuf[...] = jnp.where(_halo_row_mask(t, d), p, 0.0)
    k = cw_ref.shape[0]
    conv = None
    for j in range(k):
        s = HALO - k // 2 + j
        term = cw_ref[j:j + 1, :] * pbuf[s:s + t, :]
        conv = term if conv is None else conv + term
    gate = _dot(xbuf[HALO:HALO + t, :], wb_ref[...])
    mix = _dot((gate * conv).astype(_BF16), wo_ref[...])
    out_ref[...] = _layer_norm(alpha * hm_ref[...] + mix, g_ref[...], b_ref[...])


def _mixer_a(h, w_in, conv_w, w_out, ln_g, ln_b, alpha):
    nb, s, d = h.shape
    t = _row_tile(s)
    prev, main, nxt = _seq_specs(t, d, s // HALO)
    k = conv_w.shape[0]
    w_spec = lambda j: pl.BlockSpec((d, d), lambda b, i, j=j: (0, j), pipeline_mode=pl.Buffered(1))
    return pl.pallas_call(
        functools.partial(_mixer_a_kernel, alpha=alpha),
        grid=(nb, s // t),
        in_specs=[prev, main, nxt, w_spec(0), w_spec(1), w_spec(2), _const_spec((k, d)),
                  _const_spec((d, d)), _const_spec((1, d)), _const_spec((1, d))],
        out_specs=pl.BlockSpec((None, t, d), lambda b, i: (b, i, 0)),
        out_shape=jax.ShapeDtypeStruct((nb, s, d), _F32),
        scratch_shapes=[pltpu.VMEM((t + 2 * HALO, d), _BF16), pltpu.VMEM((t + 2 * HALO, d), _F32)],
        compiler_params=pltpu.CompilerParams(
            dimension_semantics=("parallel", "parallel"), vmem_limit_bytes=VMEM_LIMIT),
        name="mixer_a",
    )(h, h, h, w_in, w_in, w_in, conv_w, w_out, ln_g, ln_b)


def _ffn_kernel(h_ref, wg_ref, wu_ref, wd_ref, g_ref, b_ref, out_ref, *, alpha, f_chunk):
    h = h_ref[...]
    x = h.astype(_BF16)
    acc = None
    for s, n in _chunks(wg_ref.shape[1], f_chunk):
        gate = _dot(x, wg_ref[:, s:s + n])
        up = _dot(x, wu_ref[:, s:s + n])
        part = _dot((_silu(gate) * up).astype(_BF16), wd_ref[s:s + n, :])
        acc = part if acc is None else acc + part
    out_ref[...] = _layer_norm(alpha * h + acc, g_ref[...], b_ref[...])


def _ffn(h, w_gate, w_up, w_down, ln_g, ln_b, alpha):
    nb, s, d = h.shape
    f = w_gate.shape[1]
    t = _row_tile(s)
    return pl.pallas_call(
        functools.partial(_ffn_kernel, alpha=alpha, f_chunk=1024),
        grid=(nb, s // t),
        in_specs=[pl.BlockSpec((None, t, d), lambda b, i: (b, i, 0)),
                  _const_spec((d, f)), _const_spec((d, f)), _const_spec((f, d)),
                  _const_spec((1, d)), _const_spec((1, d))],
        out_specs=pl.BlockSpec((None, t, d), lambda b, i: (b, i, 0)),
        out_shape=jax.ShapeDtypeStruct((nb, s, d), _F32),
        compiler_params=pltpu.CompilerParams(
            dimension_semantics=("parallel", "parallel"), vmem_limit_bytes=VMEM_LIMIT),
        name="ffn_dense",
    )(h, w_gate, w_up, w_down, ln_g, ln_b)


def _depthwise_conv(ubuf, wbc, cbuf, t, d, k):
    off0 = HALO - k // 2
    taps = {}
    for kk in range(k):
        a, b = divmod(off0 + kk, 8)
        taps.setdefault(b, []).append((a, kk))
    bs = sorted(taps)
    a_all = sorted({a for lst in taps.values() for a, _ in lst})
    assert t % 8 == 0 and (t + 2 * HALO) // 8 >= t // 8 + a_all[-1] + 1
    n_slab = t // 8
    group = min(CONV_SLAB_GROUP, n_slab)
    rows = lax.broadcasted_iota(jnp.int32, (8, LANES), 0)

    def slab_start(m):
        return m * 8 if isinstance(m, int) else pl.multiple_of(m * 8, 8)

    def tree_sum(vals):
        while len(vals) > 1:
            vals = [vals[i] + vals[i + 1] for i in range(0, len(vals) - 1, 2)] + vals[len(vals) & ~1:]
        return vals[0]

    for c in range(d // LANES):
        lanes = pl.ds(c * LANES, LANES)

        def zslab(m, lanes=lanes):
            u = {a: ubuf[pl.ds(slab_start(m + a), 8), lanes] for a in a_all}
            return tuple(tree_sum([wbc[kk * 8:(kk + 1) * 8, lanes] * u[a] for a, kk in taps[b]])
                         for b in bs)

        def combine(zp, zn):
            return tree_sum([p if b == 0 else pltpu.roll(jnp.where(rows < b, n, p), 8 - b, 0)
                             for b, p, n in zip(bs, zp, zn)])

        def slabs(j0, count, zp, lanes=lanes):
            for g in range(count):
                zn = zslab(j0 + g + 1)
                cbuf[pl.ds(slab_start(j0 + g), 8), lanes] = combine(zp, zn)
                zp = zn
            return zp

        n_loop = n_slab // group
        zp = lax.fori_loop(0, n_loop, lambda i, z: slabs(i * group, group, z), zslab(0))
        slabs(n_loop * group, n_slab - n_loop * group, zp)


def _mixer_b_kernel(hp_ref, hm_ref, hn_ref, wa_ref, wg_ref, b1_ref, cw_ref, cb_ref, lg_ref, lb_ref,
                    w2_ref, b2_ref, g_ref, b_ref, rt_ref, out_ref, route_ref, pos_ref, cnt_ref,
                    xbuf, ubuf, cbuf, wbc, pmat, *, alpha, n_experts, cap):
    t, d = hm_ref.shape
    tp = pmat.shape[0]
    _fill_xbuf(xbuf, hp_ref, hm_ref, hn_ref, t)
    xa = xbuf[...]
    a = _dot(xa, wa_ref[...]) + b1_ref[:, 0:d]
    gl = _dot(xa, wg_ref[...]) + b1_ref[:, d:2 * d]
    ubuf[...] = jnp.where(_halo_row_mask(t, d), a * jax.nn.sigmoid(gl), 0.0)
    k = cw_ref.shape[0]
    for kk in range(k):
        wbc[kk * 8:(kk + 1) * 8, :] = jnp.broadcast_to(cw_ref[kk:kk + 1, :], (8, d))
    _depthwise_conv(ubuf, wbc, cbuf, t, d, k)
    u = _silu(_layer_norm(cbuf[...] + cb_ref[...], lg_ref[...], lb_ref[...]))
    mix = _dot(u.astype(_BF16), w2_ref[...]) + b2_ref[...]
    h1 = _layer_norm(alpha * hm_ref[...] + mix, g_ref[...], b_ref[...])
    out_ref[...] = h1

    hi = h1.astype(_BF16)
    lo = (h1 - hi.astype(_F32)).astype(_BF16)
    p_hi = _dot(hi, rt_ref[...])
    logits = p_hi[:, 0:LANES] + p_hi[:, LANES:2 * LANES] + _dot(lo, rt_ref[:, 0:LANES])
    lane = lax.broadcasted_iota(jnp.int32, logits.shape, 1).astype(_F32)
    neg = jnp.float32(-jnp.inf)
    lg1 = jnp.where(lane < n_experts, logits, neg)
    m1 = jnp.max(lg1, axis=-1, keepdims=True)
    i1 = jnp.min(jnp.where(lg1 == m1, lane, float(LANES)), axis=-1, keepdims=True)
    lg2 = jnp.where(lane == i1, neg, lg1)
    m2 = jnp.max(lg2, axis=-1, keepdims=True)
    i2 = jnp.min(jnp.where(lg2 == m2, lane, float(LANES)), axis=-1, keepdims=True)
    e2 = jnp.exp(m2 - m1)
    den = 1.0 + e2
    route = jnp.where(lane == 0.0, i1, jnp.where(lane == 1.0, i2,
                      jnp.where(lane == 2.0, 1.0 / den, jnp.where(lane == 3.0, e2 / den, 0.0))))
    route_ref[...] = route

    @pl.when(jnp.logical_and(pl.program_id(0) == 0, pl.program_id(1) == 0))
    def _():
        cnt_ref[...] = jnp.zeros_like(cnt_ref)

    member = jnp.where(jnp.logical_or(lane == i1, lane == i2), 1.0, 0.0)
    if tp > t:
        pmat[t:tp, :] = jnp.zeros((tp - t, LANES), _F32)
    pmat[0:t, :] = member
    earlier = (lax.broadcasted_iota(jnp.int32, (t, tp), 1) < lax.broadcasted_iota(jnp.int32, (t, tp), 0))
    before = _dot(jnp.where(earlier, 1.0, 0.0).astype(_BF16), pmat[...].astype(_BF16))
    slot = before + cnt_ref[...] + lane * float(cap)
    pos1 = jnp.sum(jnp.where(lane == i1, slot, 0.0), axis=-1, keepdims=True)
    pos2 = jnp.sum(jnp.where(lane == i2, slot, 0.0), axis=-1, keepdims=True)
    cnt_ref[...] += jnp.sum(member, axis=0, keepdims=True)
    pmat[0:t, :] = jnp.where(lane == 0.0, pos1, jnp.where(lane == 1.0, pos2, 0.0))
    pos_ref[...] = pmat[...].T[0:8, :].astype(jnp.int32)


def _mixer_b(h, w_pw1, b_pw1, conv_w, conv_b, cln_g, cln_b, w_pw2, b_pw2, ln_g, ln_b, router_pad,
             alpha, n_experts, cap):
    nb, s, d = h.shape
    t = _row_tile(s)
    tp = -(-t // LANES) * LANES
    prev, main, nxt = _seq_specs(t, d, s // HALO)
    k = conv_w.shape[0]
    w_spec = lambda j: pl.BlockSpec((d, d), lambda b, i, j=j: (0, j), pipeline_mode=pl.Buffered(1))
    vec = _const_spec((1, d))
    return pl.pallas_call(
        functools.partial(_mixer_b_kernel, alpha=alpha, n_experts=n_experts, cap=cap),
        grid=(nb, s // t),
        in_specs=[prev, main, nxt, w_spec(0), w_spec(1), _const_spec((1, 2 * d)),
                  _const_spec((k, d)), vec, vec, vec, _const_spec((d, d)), vec, vec, vec,
                  _const_spec((d, 2 * LANES))],
        out_specs=[pl.BlockSpec((None, t, d), lambda b, i: (b, i, 0)),
                   pl.BlockSpec((None, t, LANES), lambda b, i: (b, i, 0)),
                   pl.BlockSpec((None, None, 8, tp), lambda b, i: (b, i, 0, 0)),
                   pl.BlockSpec((1, LANES), lambda b, i: (0, 0))],
        out_shape=[jax.ShapeDtypeStruct((nb, s, d), _F32),
                   jax.ShapeDtypeStruct((nb, s, LANES), _F32),
                   jax.ShapeDtypeStruct((nb, s // t, 8, tp), jnp.int32),
                   jax.ShapeDtypeStruct((1, LANES), _F32)],
        scratch_shapes=[pltpu.VMEM((t + 2 * HALO, d), _BF16), pltpu.VMEM((t + 2 * HALO, d), _F32),
                        pltpu.VMEM((t, d), _F32), pltpu.VMEM((8 * k, d), _F32),
                        pltpu.VMEM((tp, LANES), _F32)],
        compiler_params=pltpu.CompilerParams(
            dimension_semantics=("arbitrary", "arbitrary"), vmem_limit_bytes=VMEM_LIMIT),
        name="mixer_b",
    )(h, h, h, w_pw1, w_pw1, b_pw1, conv_w, conv_b, cln_g, cln_b, w_pw2, b_pw2, ln_g, ln_b,
      router_pad)


ROWS_PER_ISSUE = 8


def _dispatch_kernel(cnt_ref, pos_ref, h_hbm, xs_hbm, zbuf, sem, zsem, *, t, cap, tm, n_experts):
    i = pl.program_id(0)
    last = pl.num_programs(0) - 1
    par = i % 2

    def zero_block(e):
        start = pl.multiple_of(e * cap + ((cnt_ref[e] + 7) // 8) * 8, 8)
        return pltpu.make_async_copy(zbuf, xs_hbm.at[pl.ds(start, tm), :], zsem)

    def zero_row(e, r):
        return pltpu.make_async_copy(zbuf.at[pl.ds(0, 1), :], xs_hbm.at[pl.ds(e * cap + r, 1), :], zsem)

    def for_zero_rows(fn):
        for e in range(n_experts):
            c = cnt_ref[e]

            def body(r, carry, e=e):
                fn(zero_row(e, r))
                return carry
            lax.fori_loop(c, ((c + 7) // 8) * 8, body, 0)

    @pl.when(i == 0)
    def _():
        zbuf[...] = jnp.zeros_like(zbuf)
        for e in range(n_experts):
            zero_block(e).start()
        for_zero_rows(lambda cp: cp.start())

    def body(q, carry):
        for u in range(ROWS_PER_ISSUE):
            r = q * ROWS_PER_ISSUE + u
            for k in range(2):
                pltpu.make_async_copy(h_hbm.at[pl.ds(i * t + r, 1), :],
                                      xs_hbm.at[pl.ds(pos_ref[0, k, r], 1), :], sem.at[par]).start()
        return carry
    lax.fori_loop(0, t // ROWS_PER_ISSUE, body, 0)

    def wait_step(s):
        for _ in range(2):
            pltpu.make_async_copy(h_hbm.at[pl.ds(0, t), :], xs_hbm.at[pl.ds(0, t), :], sem.at[s]).wait()

    @pl.when(i > 0)
    def _():
        wait_step(1 - par)

    @pl.when(i == last)
    def _():
        wait_step(par)
        for e in range(n_experts):
            zero_block(e).wait()
        for_zero_rows(lambda cp: cp.wait())


def _dispatch(h_rows, pos, counts, t, cap, tm, n_experts):
    n, d = h_rows.shape
    n_tiles, _, tp = pos.shape
    assert n_tiles * t == n and t % ROWS_PER_ISSUE == 0
    grid_spec = pltpu.PrefetchScalarGridSpec(
        num_scalar_prefetch=1,
        grid=(n_tiles,),
        in_specs=[pl.BlockSpec((1, 8, tp), lambda i, c: (i, 0, 0), memory_space=pltpu.SMEM),
                  pl.BlockSpec(memory_space=pl.ANY)],
        out_specs=pl.BlockSpec(memory_space=pl.ANY),
        scratch_shapes=[pltpu.VMEM((tm, d), _F32), pltpu.SemaphoreType.DMA((2,)),
                        pltpu.SemaphoreType.DMA(())],
    )
    return pl.pallas_call(
        functools.partial(_dispatch_kernel, t=t, cap=cap, tm=tm, n_experts=n_experts),
        grid_spec=grid_spec,
        out_shape=jax.ShapeDtypeStruct((n_experts * cap, d), _F32),
        compiler_params=pltpu.CompilerParams(
            dimension_semantics=("arbitrary",), vmem_limit_bytes=VMEM_LIMIT),
        name="moe_dispatch",
    )(counts, pos, h_rows)


def _moe_kernel(texp_ref, tblk_ref, nact_ref, x_ref, wg_ref, wu_ref, wd_ref, y_ref, *, f_chunk):
    del texp_ref, tblk_ref

    @pl.when(pl.program_id(0) < nact_ref[0])
    def _():
        x = x_ref[...].astype(_BF16)
        acc = None
        for s, n in _chunks(wg_ref.shape[1], f_chunk):
            gate = _dot(x, wg_ref[:, s:s + n])
            up = _dot(x, wu_ref[:, s:s + n])
            part = _dot((_silu(gate) * up).astype(_BF16), wd_ref[s:s + n, :])
            acc = part if acc is None else acc + part
        y_ref[...] = acc


def _tile_plan(counts, n_experts, tm, cap, m_tiles):
    tiles = (counts + tm - 1) // tm
    ends = jnp.cumsum(tiles)
    n_act = ends[-1]
    m = jnp.minimum(jnp.arange(m_tiles, dtype=jnp.int32), n_act - 1)
    te = jnp.minimum(jnp.sum((m[:, None] >= ends[None, :]).astype(jnp.int32), axis=1), n_experts - 1)
    first = jnp.sum(jnp.where(te[:, None] == jnp.arange(n_experts, dtype=jnp.int32)[None, :],
                              (ends - tiles)[None, :], 0), axis=1)
    tb = te * (cap // tm) + (m - first)
    return te.astype(jnp.int32), tb.astype(jnp.int32), n_act.reshape(1).astype(jnp.int32)


def _moe_ffn(xs, plan, w_gate, w_up, w_down, tm, f_chunk):
    rows, d = xs.shape
    fe = w_gate.shape[-1]
    te, tb, n_act = plan
    m_tiles = te.shape[0]
    w_spec = lambda shape: pl.BlockSpec((None,) + shape, lambda m, te, tb, na: (te[m], 0, 0),
                                        pipeline_mode=pl.Buffered(1))
    grid_spec = pltpu.PrefetchScalarGridSpec(
        num_scalar_prefetch=3,
        grid=(m_tiles,),
        in_specs=[pl.BlockSpec((tm, d), lambda m, te, tb, na: (tb[m], 0)),
                  w_spec((d, fe)), w_spec((d, fe)), w_spec((fe, d))],
        out_specs=pl.BlockSpec((tm, d), lambda m, te, tb, na: (tb[m], 0)),
    )
    return pl.pallas_call(
        functools.partial(_moe_kernel, f_chunk=f_chunk),
        grid_spec=grid_spec,
        out_shape=jax.ShapeDtypeStruct((rows, d), _F32),
        compiler_params=pltpu.CompilerParams(
            dimension_semantics=("arbitrary",), vmem_limit_bytes=VMEM_LIMIT),
        name="moe_ffn",
    )(te, tb, n_act, xs, w_gate, w_up, w_down)


def _combine_kernel(pos_ref, posn_ref, h_ref, route_ref, g_ref, b_ref, y_hbm, out_ref, gbuf, sem,
                    *, alpha):
    i = pl.program_id(0)
    n = pl.num_programs(0)
    t = h_ref.shape[0]
    slot = i % 2

    def start_gather(p_ref, s):
        def body(q, carry):
            for u in range(ROWS_PER_ISSUE):
                r = q * ROWS_PER_ISSUE + u
                for k in range(2):
                    pltpu.make_async_copy(y_hbm.at[pl.ds(p_ref[0, k, r], 1), :],
                                          gbuf.at[s, k, pl.ds(r, 1), :], sem.at[s]).start()
            return carry
        lax.fori_loop(0, t // ROWS_PER_ISSUE, body, 0)

    @pl.when(i == 0)
    def _():
        start_gather(pos_ref, 0)

    @pl.when(i + 1 < n)
    def _():
        start_gather(posn_ref, 1 - slot)

    pltpu.make_async_copy(gbuf.at[slot], gbuf.at[slot], sem.at[slot]).wait()
    ff = route_ref[:, 2:3] * gbuf[slot, 0] + route_ref[:, 3:4] * gbuf[slot, 1]
    out_ref[...] = _layer_norm(alpha * h_ref[...] + ff, g_ref[...], b_ref[...])


def _combine(h_rows, ys, route, pos, ln_g, ln_b, alpha, t):
    n, d = h_rows.shape
    n_tiles, _, tp = pos.shape
    assert n_tiles * t == n and t % ROWS_PER_ISSUE == 0
    pos_spec = lambda fn: pl.BlockSpec((1, 8, tp), fn, memory_space=pltpu.SMEM)
    return pl.pallas_call(
        functools.partial(_combine_kernel, alpha=alpha),
        grid=(n_tiles,),
        in_specs=[pos_spec(lambda i: (i, 0, 0)),
                  pos_spec(lambda i: (jnp.minimum(i + 1, n_tiles - 1), 0, 0)),
                  pl.BlockSpec((t, d), lambda i: (i, 0)),
                  pl.BlockSpec((t, LANES), lambda i: (i, 0)),
                  _const_spec((1, d)), _const_spec((1, d)),
                  pl.BlockSpec(memory_space=pl.ANY)],
        out_specs=pl.BlockSpec((t, d), lambda i: (i, 0)),
        out_shape=jax.ShapeDtypeStruct((n, d), _F32),
        scratch_shapes=[pltpu.VMEM((2, 2, t, d), _F32), pltpu.SemaphoreType.DMA((2,))],
        compiler_params=pltpu.CompilerParams(
            dimension_semantics=("arbitrary",), vmem_limit_bytes=VMEM_LIMIT),
        name="moe_combine",
    )(pos, pos, h_rows, route, ln_g, ln_b, ys)


def _encode(x, p, moe_tile):
    nb, seq, d = x.shape
    depth = p["ln_mix_g"].shape[0]
    alpha = (2.0 * depth) ** 0.25
    n_experts = p["moe_router"].shape[-1]
    meta = jnp.broadcast_to(p["meta_tokens"].astype(x.dtype)[None], (nb, N_META, d))
    h = jnp.concatenate([meta, x], axis=1)
    s = h.shape[1]
    row = lambda v: v.reshape(1, -1)
    for i in range(depth):
        j = i // 2
        if i % 2 == 0:
            h = _mixer_a(h, p["a_w_in"][j], p["a_conv_w"][j], p["a_w_out"][j],
                         row(p["ln_mix_g"][i]), row(p["ln_mix_b"][i]), alpha)
            h = _ffn(h, p["ffn_w_gate"][j], p["ffn_w_up"][j], p["ffn_w_down"][j],
                     row(p["ln_ffn_g"][i]), row(p["ln_ffn_b"][i]), alpha)
        else:
            router_f32 = jnp.pad(p["moe_router"][j].astype(_F32), ((0, 0), (0, LANES - n_experts)))
            router_hi = router_f32.astype(_BF16)
            router_lo = (router_f32 - router_hi.astype(_F32)).astype(_BF16)
            router_pad = jnp.concatenate([router_hi, router_lo], axis=1)
            n = nb * s
            t = _row_tile(s)
            cap = -(-n // moe_tile) * moe_tile + moe_tile
            h1, route, pos, cnt = _mixer_b(h, p["b_w_pw1"][j], row(p["b_b_pw1"][j]), p["b_conv_w"][j],
                                           row(p["b_conv_b"][j]), row(p["b_ln_g"][j]),
                                           row(p["b_ln_b"][j]), p["b_w_pw2"][j], row(p["b_b_pw2"][j]),
                                           row(p["ln_mix_g"][i]), row(p["ln_mix_b"][i]), router_pad,
                                           alpha, n_experts, cap)
            h1r = h1.reshape(n, d)
            pos = pos.reshape(n // t, 8, pos.shape[-1])
            counts = cnt[0, :n_experts].astype(jnp.int32)
            xs = _dispatch(h1r, pos, counts, t, cap, moe_tile, n_experts)
            plan = _tile_plan(counts, n_experts, moe_tile, cap, (2 * n) // moe_tile + n_experts)
            fe = p["moe_w_gate"].shape[-1]
            ys = _moe_ffn(xs, plan, p["moe_w_gate"][j], p["moe_w_up"][j], p["moe_w_down"][j],
                          moe_tile, fe // 2)
            h = _combine(h1r, ys, route.reshape(n, LANES), pos, row(p["ln_ffn_g"][i]),
                         row(p["ln_ffn_b"][i]), alpha, t).reshape(nb, s, d)
    return h[:, N_META:]


_MATMUL_WEIGHTS = ("a_w_in", "a_w_out", "b_w_pw1", "b_w_pw2", "ffn_w_gate", "ffn_w_up", "ffn_w_down",
                   "moe_w_gate", "moe_w_up", "moe_w_down")


def kernel(x_prompt, x_sample, meta_tokens, a_w_in, a_conv_w, a_w_out, b_w_pw1, b_b_pw1, b_conv_w,
           b_conv_b, b_ln_g, b_ln_b, b_w_pw2, b_b_pw2, ffn_w_gate, ffn_w_up, ffn_w_down, moe_router,
           moe_w_gate, moe_w_up, moe_w_down, ln_mix_g, ln_mix_b, ln_ffn_g, ln_ffn_b):
    p = dict(meta_tokens=meta_tokens, a_w_in=a_w_in, a_conv_w=a_conv_w, a_w_out=a_w_out,
             b_w_pw1=b_w_pw1, b_b_pw1=b_b_pw1, b_conv_w=b_conv_w, b_conv_b=b_conv_b, b_ln_g=b_ln_g,
             b_ln_b=b_ln_b, b_w_pw2=b_w_pw2, b_b_pw2=b_b_pw2, ffn_w_gate=ffn_w_gate,
             ffn_w_up=ffn_w_up, ffn_w_down=ffn_w_down, moe_router=moe_router, moe_w_gate=moe_w_gate,
             moe_w_up=moe_w_up, moe_w_down=moe_w_down, ln_mix_g=ln_mix_g, ln_mix_b=ln_mix_b,
             ln_ffn_g=ln_ffn_g, ln_ffn_b=ln_ffn_b)
    for name in _MATMUL_WEIGHTS:
        p[name] = p[name].astype(_BF16)
    return (_encode(x_prompt, p, MOE_ROW_TILE), _encode(x_sample, p, MOE_ROW_TILE))
```

```python
import functools

import jax
import jax.numpy as jnp
from jax import lax
from jax.experimental import pallas as pl
from jax.experimental.pallas import tpu as pltpu

N_META = 16
LN_EPS = 1e-5
HALO = 16
LANES = 128
BF16_ROWS = 16
MAX_ROW_TILE = 1024
MOE_ROW_TILE = 512
CONV_SLAB_GROUP = 4
VMEM_LIMIT = 56 * 1024 * 1024

_BF16 = jnp.bfloat16
_F32 = jnp.float32


def _row_tile(rows):
    best = None
    for t in range(BF16_ROWS, MAX_ROW_TILE + 1, BF16_ROWS):
        if rows % t == 0:
            best = t
    assert best is not None, rows
    return best


def _chunks(total, size):
    out, s = [], 0
    while s < total:
        n = min(size, total - s)
        out.append((s, n))
        s += n
    return out


def _dot(a, b):
    return jnp.dot(a, b, preferred_element_type=_F32)


def _layer_norm(x, g, b):
    mu = jnp.mean(x, axis=-1, keepdims=True)
    xc = x - mu
    var = jnp.mean(xc * xc, axis=-1, keepdims=True)
    return xc * lax.rsqrt(var + LN_EPS) * g + b


def _silu(x):
    return x * jax.nn.sigmoid(x)


def _const_spec(shape):
    nd = len(shape)
    return pl.BlockSpec(shape, lambda *_: (0,) * nd, pipeline_mode=pl.Buffered(1))


def _seq_specs(t, d, n_halo_blocks):
    per_tile = t // HALO
    main = pl.BlockSpec((None, t, d), lambda b, i: (b, i, 0))
    prev = pl.BlockSpec((None, HALO, d), lambda b, i: (b, jnp.maximum(i * per_tile - 1, 0), 0))
    nxt = pl.BlockSpec((None, HALO, d),
                       lambda b, i: (b, jnp.minimum((i + 1) * per_tile, n_halo_blocks - 1), 0))
    return prev, main, nxt


def _fill_xbuf(xbuf, hp_ref, hm_ref, hn_ref, t):
    xbuf[0:HALO, :] = hp_ref[...].astype(_BF16)
    xbuf[HALO:HALO + t, :] = hm_ref[...].astype(_BF16)
    xbuf[HALO + t:HALO + t + HALO, :] = hn_ref[...].astype(_BF16)


def _halo_row_mask(t, d):
    i = pl.program_id(1)
    last = pl.num_programs(1) - 1
    r = lax.broadcasted_iota(jnp.int32, (t + 2 * HALO, d), 0)
    return jnp.logical_and(jnp.logical_or(r >= HALO, i > 0),
                           jnp.logical_or(r < HALO + t, i < last))


def _mixer_a_kernel(hp_ref, hm_ref, hn_ref, wb_ref, wc_ref, wh_ref, cw_ref, wo_ref,
                    g_ref, b_ref, out_ref, xbuf, pbuf, *, alpha):
    t, d = hm_ref.shape
    _fill_xbuf(xbuf, hp_ref, hm_ref, hn_ref, t)
    xa = xbuf[...]
    p = _dot(xa, wc_ref[...]) * _dot(xa, wh_ref[...])
    pbuf[...] = jnp.where(_halo_row_mask(t, d), p, 0.0)
    k = cw_ref.shape[0]
    conv = None
    for j in range(k):
        s = HALO - k // 2 + j
        term = cw_ref[j:j + 1, :] * pbuf[s:s + t, :]
        conv = term if conv is None else conv + term
    gate = _dot(xbuf[HALO:HALO + t, :], wb_ref[...])
    mix = _dot((gate * conv).astype(_BF16), wo_ref[...])
    out_ref[...] = _layer_norm(alpha * hm_ref[...] + mix, g_ref[...], b_ref[...])


def _mixer_a(h, w_in, conv_w, w_out, ln_g, ln_b, alpha):
    nb, s, d = h.shape
    t = _row_tile(s)
    prev, main, nxt = _seq_specs(t, d, s // HALO)
    k = conv_w.shape[0]
    w_spec = lambda j: pl.BlockSpec((d, d), lambda b, i, j=j: (0, j), pipeline_mode=pl.Buffered(1))
    return pl.pallas_call(
        functools.partial(_mixer_a_kernel, alpha=alpha),
        grid=(nb, s // t),
        in_specs=[prev, main, nxt, w_spec(0), w_spec(1), w_spec(2), _const_spec((k, d)),
                  _const_spec((d, d)), _const_spec((1, d)), _const_spec((1, d))],
        out_specs=pl.BlockSpec((None, t, d), lambda b, i: (b, i, 0)),
        out_shape=jax.ShapeDtypeStruct((nb, s, d), _F32),
        scratch_shapes=[pltpu.VMEM((t + 2 * HALO, d), _BF16), pltpu.VMEM((t + 2 * HALO, d), _F32)],
        compiler_params=pltpu.CompilerParams(
            dimension_semantics=("parallel", "parallel"), vmem_limit_bytes=VMEM_LIMIT),
        name="mixer_a",
    )(h, h, h, w_in, w_in, w_in, conv_w, w_out, ln_g, ln_b)


def _ffn_kernel(h_ref, wg_ref, wu_ref, wd_ref, g_ref, b_ref, out_ref, *, alpha, f_chunk):
    h = h_ref[...]
    x = h.astype(_BF16)
    acc = None
    for s, n in _chunks(wg_ref.shape[1], f_chunk):
        gate = _dot(x, wg_ref[:, s:s + n])
        up = _dot(x, wu_ref[:, s:s + n])
        part = _dot((_silu(gate) * up).astype(_BF16), wd_ref[s:s + n, :])
        acc = part if acc is None else acc + part
    out_ref[...] = _layer_norm(alpha * h + acc, g_ref[...], b_ref[...])


def _ffn(h, w_gate, w_up, w_down, ln_g, ln_b, alpha):
    nb, s, d = h.shape
    f = w_gate.shape[1]
    t = _row_tile(s)
    return pl.pallas_call(
        functools.partial(_ffn_kernel, alpha=alpha, f_chunk=1024),
        grid=(nb, s // t),
        in_specs=[pl.BlockSpec((None, t, d), lambda b, i: (b, i, 0)),
                  _const_spec((d, f)), _const_spec((d, f)), _const_spec((f, d)),
                  _const_spec((1, d)), _const_spec((1, d))],
        out_specs=pl.BlockSpec((None, t, d), lambda b, i: (b, i, 0)),
        out_shape=jax.ShapeDtypeStruct((nb, s, d), _F32),
        compiler_params=pltpu.CompilerParams(
            dimension_semantics=("parallel", "parallel"), vmem_limit_bytes=VMEM_LIMIT),
        name="ffn_dense",
    )(h, w_gate, w_up, w_down, ln_g, ln_b)


def _depthwise_conv(ubuf, wbc, cbuf, t, d, k):
    off0 = HALO - k // 2
    taps = {}
    for kk in range(k):
        a, b = divmod(off0 + kk, 8)
        taps.setdefault(b, []).append((a, kk))
    bs = sorted(taps)
    a_all = sorted({a for lst in taps.values() for a, _ in lst})
    assert t % 8 == 0 and (t + 2 * HALO) // 8 >= t // 8 + a_all[-1] + 1
    n_slab = t // 8
    group = min(CONV_SLAB_GROUP, n_slab)
    rows = lax.broadcasted_iota(jnp.int32, (8, LANES), 0)

    def slab_start(m):
        return m * 8 if isinstance(m, int) else pl.multiple_of(m * 8, 8)

    def tree_sum(vals):
        while len(vals) > 1:
            vals = [vals[i] + vals[i + 1] for i in range(0, len(vals) - 1, 2)] + vals[len(vals) & ~1:]
        return vals[0]

    for c in range(d // LANES):
        lanes = pl.ds(c * LANES, LANES)

        def zslab(m, lanes=lanes):
            u = {a: ubuf[pl.ds(slab_start(m + a), 8), lanes] for a in a_all}
            return tuple(tree_sum([wbc[kk * 8:(kk + 1) * 8, lanes] * u[a] for a, kk in taps[b]])
                         for b in bs)

        def combine(zp, zn):
            return tree_sum([p if b == 0 else pltpu.roll(jnp.where(rows < b, n, p), 8 - b, 0)
                             for b, p, n in zip(bs, zp, zn)])

        def slabs(j0, count, zp, lanes=lanes):
            for g in range(count):
                zn = zslab(j0 + g + 1)
                cbuf[pl.ds(slab_start(j0 + g), 8), lanes] = combine(zp, zn)
                zp = zn
            return zp

        n_loop = n_slab // group
        zp = lax.fori_loop(0, n_loop, lambda i, z: slabs(i * group, group, z), zslab(0))
        slabs(n_loop * group, n_slab - n_loop * group, zp)


def _mixer_b_kernel(hp_ref, hm_ref, hn_ref, wa_ref, wg_ref, b1_ref, cw_ref, cb_ref, lg_ref, lb_ref,
                    w2_ref, b2_ref, g_ref, b_ref, rt_ref, out_ref, route_ref, pos_ref, cnt_ref,
                    xbuf, ubuf, cbuf, wbc, pmat, *, alpha, n_experts, cap):
    t, d = hm_ref.shape
    tp = pmat.shape[0]
    _fill_xbuf(xbuf, hp_ref, hm_ref, hn_ref, t)
    xa = xbuf[...]
    a = _dot(xa, wa_ref[...]) + b1_ref[:, 0:d]
    gl = _dot(xa, wg_ref[...]) + b1_ref[:, d:2 * d]
    ubuf[...] = jnp.where(_halo_row_mask(t, d), a * jax.nn.sigmoid(gl), 0.0)
    k = cw_ref.shape[0]
    for kk in range(k):
        wbc[kk * 8:(kk + 1) * 8, :] = jnp.broadcast_to(cw_ref[kk:kk + 1, :], (8, d))
    _depthwise_conv(ubuf, wbc, cbuf, t, d, k)
    u = _silu(_layer_norm(cbuf[...] + cb_ref[...], lg_ref[...], lb_ref[...]))
    mix = _dot(u.astype(_BF16), w2_ref[...]) + b2_ref[...]
    h1 = _layer_norm(alpha * hm_ref[...] + mix, g_ref[...], b_ref[...])
    out_ref[...] = h1

    hi = h1.astype(_BF16)
    lo = (h1 - hi.astype(_F32)).astype(_BF16)
    p_hi = _dot(hi, rt_ref[...])
    logits = p_hi[:, 0:LANES] + p_hi[:, LANES:2 * LANES] + _dot(lo, rt_ref[:, 0:LANES])
    lane = lax.broadcasted_iota(jnp.int32, logits.shape, 1).astype(_F32)
    neg = jnp.float32(-jnp.inf)
    lg1 = jnp.where(lane < n_experts, logits, neg)
    m1 = jnp.max(lg1, axis=-1, keepdims=True)
    i1 = jnp.min(jnp.where(lg1 == m1, lane, float(LANES)), axis=-1, keepdims=True)
    lg2 = jnp.where(lane == i1, neg, lg1)
    m2 = jnp.max(lg2, axis=-1, keepdims=True)
    i2 = jnp.min(jnp.where(lg2 == m2, lane, float(LANES)), axis=-1, keepdims=True)
    e2 = jnp.exp(m2 - m1)
    den = 1.0 + e2
    route = jnp.where(lane == 0.0, i1, jnp.where(lane == 1.0, i2,
                      jnp.where(lane == 2.0, 1.0 / den, jnp.where(lane == 3.0, e2 / den, 0.0))))
    route_ref[...] = route

    @pl.when(jnp.logical_and(pl.program_id(0) == 0, pl.program_id(1) == 0))
    def _():
        cnt_ref[...] = jnp.zeros_like(cnt_ref)

    member = jnp.where(jnp.logical_or(lane == i1, lane == i2), 1.0, 0.0)
    if tp > t:
        pmat[t:tp, :] = jnp.zeros((tp - t, LANES), _F32)
    pmat[0:t, :] = member
    earlier = (lax.broadcasted_iota(jnp.int32, (t, tp), 1) < lax.broadcasted_iota(jnp.int32, (t, tp), 0))
    before = _dot(jnp.where(earlier, 1.0, 0.0).astype(_BF16), pmat[...].astype(_BF16))
    slot = before + cnt_ref[...] + lane * float(cap)
    pos1 = jnp.sum(jnp.where(lane == i1, slot, 0.0), axis=-1, keepdims=True)
    pos2 = jnp.sum(jnp.where(lane == i2, slot, 0.0), axis=-1, keepdims=True)
    cnt_ref[...] += jnp.sum(member, axis=0, keepdims=True)
    pmat[0:t, :] = jnp.where(lane == 0.0, pos1, jnp.where(lane == 1.0, pos2, 0.0))
    pos_ref[...] = pmat[...].T[0:8, :].astype(jnp.int32)


def _mixer_b(h, w_pw1, b_pw1, conv_w, conv_b, cln_g, cln_b, w_pw2, b_pw2, ln_g, ln_b, router_pad,
             alpha, n_experts, cap):
    nb, s, d = h.shape
    t = _row_tile(s)
    tp = -(-t // LANES) * LANES
    prev, main, nxt = _seq_specs(t, d, s // HALO)
    k = conv_w.shape[0]
    w_spec = lambda j: pl.BlockSpec((d, d), lambda b, i, j=j: (0, j), pipeline_mode=pl.Buffered(1))
    vec = _const_spec((1, d))
    return pl.pallas_call(
        functools.partial(_mixer_b_kernel, alpha=alpha, n_experts=n_experts, cap=cap),
        grid=(nb, s // t),
        in_specs=[prev, main, nxt, w_spec(0), w_spec(1), _const_spec((1, 2 * d)),
                  _const_spec((k, d)), vec, vec, vec, _const_spec((d, d)), vec, vec, vec,
                  _const_spec((d, 2 * LANES))],
        out_specs=[pl.BlockSpec((None, t, d), lambda b, i: (b, i, 0)),
                   pl.BlockSpec((None, t, LANES), lambda b, i: (b, i, 0)),
                   pl.BlockSpec((None, None, 8, tp), lambda b, i: (b, i, 0, 0)),
                   pl.BlockSpec((1, LANES), lambda b, i: (0, 0))],
        out_shape=[jax.ShapeDtypeStruct((nb, s, d), _F32),
                   jax.ShapeDtypeStruct((nb, s, LANES), _F32),
                   jax.ShapeDtypeStruct((nb, s // t, 8, tp), jnp.int32),
                   jax.ShapeDtypeStruct((1, LANES), _F32)],
        scratch_shapes=[pltpu.VMEM((t + 2 * HALO, d), _BF16), pltpu.VMEM((t + 2 * HALO, d), _F32),
                        pltpu.VMEM((t, d), _F32), pltpu.VMEM((8 * k, d), _F32),
                        pltpu.VMEM((tp, LANES), _F32)],
        compiler_params=pltpu.CompilerParams(
            dimension_semantics=("arbitrary", "arbitrary"), vmem_limit_bytes=VMEM_LIMIT),
        name="mixer_b",
    )(h, h, h, w_pw1, w_pw1, b_pw1, conv_w, conv_b, cln_g, cln_b, w_pw2, b_pw2, ln_g, ln_b,
      router_pad)


ROWS_PER_ISSUE = 8


def _rows_to_tiles(dst_ref, x):
    n, d = x.shape
    c = d // LANES
    for j in range(c):
        dst_ref[pl.ds(j, n, stride=c), :] = x[:, j * LANES:(j + 1) * LANES]


def _tiles_to_rows(src_ref, n, c):
    return jnp.concatenate([src_ref[pl.ds(j, n, stride=c), :] for j in range(c)], axis=1)


def _token_tile(ref, token, c):
    start = token * c
    if c % 8 == 0:
        start = pl.multiple_of(start, 8)
    return ref.at[pl.ds(start, c), :]


def _dispatch_kernel(cnt_ref, pos_ref, h_ref, xs_hbm, tbuf, zbuf, sem, zsem, *, cap, tm, n_experts):
    i = pl.program_id(0)
    last = pl.num_programs(0) - 1
    slot = i % 2
    t, d = h_ref.shape
    c = d // LANES

    def zero_block(e):
        start = (e * cap + cnt_ref[e]) * c
        if c % 8 == 0:
            start = pl.multiple_of(start, 8)
        return pltpu.make_async_copy(zbuf, xs_hbm.at[pl.ds(start, tm * c), :], zsem)

    def wait_step(s):
        for _ in range(2):
            pltpu.make_async_copy(tbuf.at[s], tbuf.at[s], sem.at[s]).wait()

    @pl.when(i == 0)
    def _():
        zbuf[...] = jnp.zeros_like(zbuf)
        for e in range(n_experts):
            zero_block(e).start()

    @pl.when(i >= 2)
    def _():
        wait_step(slot)

    _rows_to_tiles(tbuf.at[slot], h_ref[...])

    def body(q, carry):
        for u in range(ROWS_PER_ISSUE):
            r = q * ROWS_PER_ISSUE + u
            for k in range(2):
                pltpu.make_async_copy(_token_tile(tbuf.at[slot], r, c),
                                      _token_tile(xs_hbm, pos_ref[0, k, r], c), sem.at[slot]).start()
        return carry
    lax.fori_loop(0, t // ROWS_PER_ISSUE, body, 0)

    @pl.when(i == last)
    def _():
        wait_step(slot)

        @pl.when(i >= 1)
        def _():
            wait_step(1 - slot)
        for e in range(n_experts):
            zero_block(e).wait()


def _dispatch(h_rows, pos, counts, t, cap, tm, n_experts):
    n, d = h_rows.shape
    c = d // LANES
    n_tiles, _, tp = pos.shape
    assert n_tiles * t == n and t % ROWS_PER_ISSUE == 0 and c * LANES == d
    grid_spec = pltpu.PrefetchScalarGridSpec(
        num_scalar_prefetch=1,
        grid=(n_tiles,),
        in_specs=[pl.BlockSpec((1, 8, tp), lambda i, cnt: (i, 0, 0), memory_space=pltpu.SMEM),
                  pl.BlockSpec((t, d), lambda i, cnt: (i, 0))],
        out_specs=pl.BlockSpec(memory_space=pl.ANY),
        scratch_shapes=[pltpu.VMEM((2, t * c, LANES), _F32), pltpu.VMEM((tm * c, LANES), _F32),
                        pltpu.SemaphoreType.DMA((2,)), pltpu.SemaphoreType.DMA(())],
    )
    return pl.pallas_call(
        functools.partial(_dispatch_kernel, cap=cap, tm=tm, n_experts=n_experts),
        grid_spec=grid_spec,
        out_shape=jax.ShapeDtypeStruct((n_experts * cap * c, LANES), _F32),
        compiler_params=pltpu.CompilerParams(
            dimension_semantics=("arbitrary",), vmem_limit_bytes=VMEM_LIMIT),
        name="moe_dispatch",
    )(counts, pos, h_rows)


def _moe_kernel(texp_ref, tblk_ref, nact_ref, x_ref, wg_ref, wu_ref, wd_ref, y_ref, *, f_chunk):
    del texp_ref, tblk_ref
    c = wg_ref.shape[0] // LANES
    tm = x_ref.shape[0] // c

    @pl.when(pl.program_id(0) < nact_ref[0])
    def _():
        x = _tiles_to_rows(x_ref, tm, c).astype(_BF16)
        acc = None
        for s, n in _chunks(wg_ref.shape[1], f_chunk):
            gate = _dot(x, wg_ref[:, s:s + n])
            up = _dot(x, wu_ref[:, s:s + n])
            part = _dot((_silu(gate) * up).astype(_BF16), wd_ref[s:s + n, :])
            acc = part if acc is None else acc + part
        _rows_to_tiles(y_ref, acc)


def _tile_plan(counts, n_experts, tm, cap, m_tiles):
    tiles = (counts + tm - 1) // tm
    ends = jnp.cumsum(tiles)
    n_act = ends[-1]
    m = jnp.minimum(jnp.arange(m_tiles, dtype=jnp.int32), n_act - 1)
    te = jnp.minimum(jnp.sum((m[:, None] >= ends[None, :]).astype(jnp.int32), axis=1), n_experts - 1)
    first = jnp.sum(jnp.where(te[:, None] == jnp.arange(n_experts, dtype=jnp.int32)[None, :],
                              (ends - tiles)[None, :], 0), axis=1)
    tb = te * (cap // tm) + (m - first)
    return te.astype(jnp.int32), tb.astype(jnp.int32), n_act.reshape(1).astype(jnp.int32)


def _moe_ffn(xs, plan, w_gate, w_up, w_down, tm, f_chunk):
    _, d, fe = w_gate.shape
    c = d // LANES
    te, tb, n_act = plan
    m_tiles = te.shape[0]
    w_spec = lambda shape: pl.BlockSpec((None,) + shape, lambda m, te, tb, na: (te[m], 0, 0),
                                        pipeline_mode=pl.Buffered(1))
    grid_spec = pltpu.PrefetchScalarGridSpec(
        num_scalar_prefetch=3,
        grid=(m_tiles,),
        in_specs=[pl.BlockSpec((tm * c, LANES), lambda m, te, tb, na: (tb[m], 0)),
                  w_spec((d, fe)), w_spec((d, fe)), w_spec((fe, d))],
        out_specs=pl.BlockSpec((tm * c, LANES), lambda m, te, tb, na: (tb[m], 0)),
    )
    return pl.pallas_call(
        functools.partial(_moe_kernel, f_chunk=f_chunk),
        grid_spec=grid_spec,
        out_shape=jax.ShapeDtypeStruct(xs.shape, _F32),
        compiler_params=pltpu.CompilerParams(
            dimension_semantics=("arbitrary",), vmem_limit_bytes=VMEM_LIMIT),
        name="moe_ffn",
    )(te, tb, n_act, xs, w_gate, w_up, w_down)


def _combine_kernel(pos_ref, posn_ref, h_ref, route_ref, g_ref, b_ref, y_hbm, out_ref, gbuf, sem,
                    *, alpha):
    i = pl.program_id(0)
    n = pl.num_programs(0)
    t, d = h_ref.shape
    c = d // LANES
    slot = i % 2

    def start_gather(p_ref, s):
        def body(q, carry):
            for u in range(ROWS_PER_ISSUE):
                r = q * ROWS_PER_ISSUE + u
                for k in range(2):
                    pltpu.make_async_copy(_token_tile(y_hbm, p_ref[0, k, r], c),
                                          _token_tile(gbuf.at[s, k], r, c), sem.at[s]).start()
            return carry
        lax.fori_loop(0, t // ROWS_PER_ISSUE, body, 0)

    @pl.when(i == 0)
    def _():
        start_gather(pos_ref, 0)

    @pl.when(i + 1 < n)
    def _():
        start_gather(posn_ref, 1 - slot)

    pltpu.make_async_copy(gbuf.at[slot], gbuf.at[slot], sem.at[slot]).wait()
    ff = (route_ref[:, 2:3] * _tiles_to_rows(gbuf.at[slot, 0], t, c)
          + route_ref[:, 3:4] * _tiles_to_rows(gbuf.at[slot, 1], t, c))
    out_ref[...] = _layer_norm(alpha * h_ref[...] + ff, g_ref[...], b_ref[...])


def _combine(h_rows, ys, route, pos, ln_g, ln_b, alpha, t):
    n, d = h_rows.shape
    n_tiles, _, tp = pos.shape
    assert n_tiles * t == n and t % ROWS_PER_ISSUE == 0
    pos_spec = lambda fn: pl.BlockSpec((1, 8, tp), fn, memory_space=pltpu.SMEM)
    return pl.pallas_call(
        functools.partial(_combine_kernel, alpha=alpha),
        grid=(n_tiles,),
        in_specs=[pos_spec(lambda i: (i, 0, 0)),
                  pos_spec(lambda i: (jnp.minimum(i + 1, n_tiles - 1), 0, 0)),
                  pl.BlockSpec((t, d), lambda i: (i, 0)),
                  pl.BlockSpec((t, LANES), lambda i: (i, 0)),
                  _const_spec((1, d)), _const_spec((1, d)),
                  pl.BlockSpec(memory_space=pl.ANY)],
        out_specs=pl.BlockSpec((t, d), lambda i: (i, 0)),
        out_shape=jax.ShapeDtypeStruct((n, d), _F32),
        scratch_shapes=[pltpu.VMEM((2, 2, t * (d // LANES), LANES), _F32), pltpu.SemaphoreType.DMA((2,))],
        compiler_params=pltpu.CompilerParams(
            dimension_semantics=("arbitrary",), vmem_limit_bytes=VMEM_LIMIT),
        name="moe_combine",
    )(pos, pos, h_rows, route, ln_g, ln_b, ys)


def _encode(x, p, moe_tile):
    nb, seq, d = x.shape
    depth = p["ln_mix_g"].shape[0]
    alpha = (2.0 * depth) ** 0.25
    n_experts = p["moe_router"].shape[-1]
    meta = jnp.broadcast_to(p["meta_tokens"].astype(x.dtype)[None], (nb, N_META, d))
    h = jnp.concatenate([meta, x], axis=1)
    s = h.shape[1]
    row = lambda v: v.reshape(1, -1)
    for i in range(depth):
        j = i // 2
        if i % 2 == 0:
            h = _mixer_a(h, p["a_w_in"][j], p["a_conv_w"][j], p["a_w_out"][j],
                         row(p["ln_mix_g"][i]), row(p["ln_mix_b"][i]), alpha)
            h = _ffn(h, p["ffn_w_gate"][j], p["ffn_w_up"][j], p["ffn_w_down"][j],
                     row(p["ln_ffn_g"][i]), row(p["ln_ffn_b"][i]), alpha)
        else:
            router_f32 = jnp.pad(p["moe_router"][j].astype(_F32), ((0, 0), (0, LANES - n_experts)))
            router_hi = router_f32.astype(_BF16)
            router_lo = (router_f32 - router_hi.astype(_F32)).astype(_BF16)
            router_pad = jnp.concatenate([router_hi, router_lo], axis=1)
            n = nb * s
            t = _row_tile(s)
            cap = -(-n // moe_tile) * moe_tile + moe_tile
            h1, route, pos, cnt = _mixer_b(h, p["b_w_pw1"][j], row(p["b_b_pw1"][j]), p["b_conv_w"][j],
                                           row(p["b_conv_b"][j]), row(p["b_ln_g"][j]),
                                           row(p["b_ln_b"][j]), p["b_w_pw2"][j], row(p["b_b_pw2"][j]),
                                           row(p["ln_mix_g"][i]), row(p["ln_mix_b"][i]), router_pad,
                                           alpha, n_experts, cap)
            h1r = h1.reshape(n, d)
            pos = pos.reshape(n // t, 8, pos.shape[-1])
            counts = cnt[0, :n_experts].astype(jnp.int32)
            xs = _dispatch(h1r, pos, counts, t, cap, moe_tile, n_experts)
            plan = _tile_plan(counts, n_experts, moe_tile, cap, (2 * n) // moe_tile + n_experts)
            fe = p["moe_w_gate"].shape[-1]
            ys = _moe_ffn(xs, plan, p["moe_w_gate"][j], p["moe_w_up"][j], p["moe_w_down"][j],
                          moe_tile, fe // 2)
            h = _combine(h1r, ys, route.reshape(n, LANES), pos, row(p["ln_ffn_g"][i]),
                         row(p["ln_ffn_b"][i]), alpha, t).reshape(nb, s, d)
    return h[:, N_META:]


_MATMUL_WEIGHTS = ("a_w_in", "a_w_out", "b_w_pw1", "b_w_pw2", "ffn_w_gate", "ffn_w_up", "ffn_w_down",
                   "moe_w_gate", "moe_w_up", "moe_w_down")


def kernel(x_prompt, x_sample, meta_tokens, a_w_in, a_conv_w, a_w_out, b_w_pw1, b_b_pw1, b_conv_w,
           b_conv_b, b_ln_g, b_ln_b, b_w_pw2, b_b_pw2, ffn_w_gate, ffn_w_up, ffn_w_down, moe_router,
           moe_w_gate, moe_w_up, moe_w_down, ln_mix_g, ln_mix_b, ln_ffn_g, ln_ffn_b):
    p = dict(meta_tokens=meta_tokens, a_w_in=a_w_in, a_conv_w=a_conv_w, a_w_out=a_w_out,
             b_w_pw1=b_w_pw1, b_b_pw1=b_b_pw1, b_conv_w=b_conv_w, b_conv_b=b_conv_b, b_ln_g=b_ln_g,
             b_ln_b=b_ln_b, b_w_pw2=b_w_pw2, b_b_pw2=b_b_pw2, ffn_w_gate=ffn_w_gate,
             ffn_w_up=ffn_w_up, ffn_w_down=ffn_w_down, moe_router=moe_router, moe_w_gate=moe_w_gate,
             moe_w_up=moe_w_up, moe_w_down=moe_w_down, ln_mix_g=ln_mix_g, ln_mix_b=ln_mix_b,
             ln_ffn_g=ln_ffn_g, ln_ffn_b=ln_ffn_b)
    for name in _MATMUL_WEIGHTS:
        p[name] = p[name].astype(_BF16)
    return (_encode(x_prompt, p, MOE_ROW_TILE), _encode(x_sample, p, MOE_ROW_TILE))
```

```python
import functools

import jax
import jax.numpy as jnp
from jax import lax
from jax.experimental import pallas as pl
from jax.experimental.pallas import tpu as pltpu

N_META = 16
LN_EPS = 1e-5
HALO = 16
LANES = 128
BF16_ROWS = 16
MAX_ROW_TILE = 1024
MOE_ROW_TILE = 512
CONV_SLAB_GROUP = 6
VMEM_LIMIT = 56 * 1024 * 1024

_BF16 = jnp.bfloat16
_F32 = jnp.float32


def _row_tile(rows):
    best = None
    for t in range(BF16_ROWS, MAX_ROW_TILE + 1, BF16_ROWS):
        if rows % t == 0:
            best = t
    assert best is not None, rows
    return best


def _chunks(total, size):
    out, s = [], 0
    while s < total:
        n = min(size, total - s)
        out.append((s, n))
        s += n
    return out


def _dot(a, b):
    return jnp.dot(a, b, preferred_element_type=_F32)


def _layer_norm(x, g, b):
    mu = jnp.mean(x, axis=-1, keepdims=True)
    xc = x - mu
    var = jnp.mean(xc * xc, axis=-1, keepdims=True)
    return xc * lax.rsqrt(var + LN_EPS) * g + b


def _silu(x):
    return x * jax.nn.sigmoid(x)


def _const_spec(shape):
    nd = len(shape)
    return pl.BlockSpec(shape, lambda *_: (0,) * nd, pipeline_mode=pl.Buffered(1))


def _seq_specs(t, d, n_halo_blocks):
    per_tile = t // HALO
    main = pl.BlockSpec((None, t, d), lambda b, i: (b, i, 0))
    prev = pl.BlockSpec((None, HALO, d), lambda b, i: (b, jnp.maximum(i * per_tile - 1, 0), 0))
    nxt = pl.BlockSpec((None, HALO, d),
                       lambda b, i: (b, jnp.minimum((i + 1) * per_tile, n_halo_blocks - 1), 0))
    return prev, main, nxt


def _fill_xbuf(xbuf, hp_ref, hm_ref, hn_ref, t):
    xbuf[0:HALO, :] = hp_ref[...].astype(_BF16)
    xbuf[HALO:HALO + t, :] = hm_ref[...].astype(_BF16)
    xbuf[HALO + t:HALO + t + HALO, :] = hn_ref[...].astype(_BF16)


def _halo_row_mask(t, d):
    i = pl.program_id(1)
    last = pl.num_programs(1) - 1
    r = lax.broadcasted_iota(jnp.int32, (t + 2 * HALO, d), 0)
    return jnp.logical_and(jnp.logical_or(r >= HALO, i > 0),
                           jnp.logical_or(r < HALO + t, i < last))


def _mixer_a_kernel(hp_ref, hm_ref, hn_ref, wb_ref, wc_ref, wh_ref, cw_ref, wo_ref,
                    g_ref, b_ref, out_ref, xbuf, pbuf, *, alpha):
    t, d = hm_ref.shape
    _fill_xbuf(xbuf, hp_ref, hm_ref, hn_ref, t)
    xa = xbuf[...]
    p = _dot(xa, wc_ref[...]) * _dot(xa, wh_ref[...])
    pbuf[...] = jnp.where(_halo_row_mask(t, d), p, 0.0)
    k = cw_ref.shape[0]
    conv = None
    for j in range(k):
        s = HALO - k // 2 + j
        term = cw_ref[j:j + 1, :] * pbuf[s:s + t, :]
        conv = term if conv is None else conv + term
    gate = _dot(xbuf[HALO:HALO + t, :], wb_ref[...])
    mix = _dot((gate * conv).astype(_BF16), wo_ref[...])
    out_ref[...] = _layer_norm(alpha * hm_ref[...] + mix, g_ref[...], b_ref[...])


def _mixer_a(h, w_in, conv_w, w_out, ln_g, ln_b, alpha):
    nb, s, d = h.shape
    t = _row_tile(s)
    prev, main, nxt = _seq_specs(t, d, s // HALO)
    k = conv_w.shape[0]
    w_spec = lambda j: pl.BlockSpec((d, d), lambda b, i, j=j: (0, j), pipeline_mode=pl.Buffered(1))
    return pl.pallas_call(
        functools.partial(_mixer_a_kernel, alpha=alpha),
        grid=(nb, s // t),
        in_specs=[prev, main, nxt, w_spec(0), w_spec(1), w_spec(2), _const_spec((k, d)),
                  _const_spec((d, d)), _const_spec((1, d)), _const_spec((1, d))],
        out_specs=pl.BlockSpec((None, t, d), lambda b, i: (b, i, 0)),
        out_shape=jax.ShapeDtypeStruct((nb, s, d), _F32),
        scratch_shapes=[pltpu.VMEM((t + 2 * HALO, d), _BF16), pltpu.VMEM((t + 2 * HALO, d), _F32)],
        compiler_params=pltpu.CompilerParams(
            dimension_semantics=("parallel", "parallel"), vmem_limit_bytes=VMEM_LIMIT),
        name="mixer_a",
    )(h, h, h, w_in, w_in, w_in, conv_w, w_out, ln_g, ln_b)


def _ffn_kernel(h_ref, wg_ref, wu_ref, wd_ref, g_ref, b_ref, out_ref, *, alpha, f_chunk):
    h = h_ref[...]
    x = h.astype(_BF16)
    acc = None
    for s, n in _chunks(wg_ref.shape[1], f_chunk):
        gate = _dot(x, wg_ref[:, s:s + n])
        up = _dot(x, wu_ref[:, s:s + n])
        part = _dot((_silu(gate) * up).astype(_BF16), wd_ref[s:s + n, :])
        acc = part if acc is None else acc + part
    out_ref[...] = _layer_norm(alpha * h + acc, g_ref[...], b_ref[...])


def _ffn(h, w_gate, w_up, w_down, ln_g, ln_b, alpha):
    nb, s, d = h.shape
    f = w_gate.shape[1]
    t = _row_tile(s)
    return pl.pallas_call(
        functools.partial(_ffn_kernel, alpha=alpha, f_chunk=1024),
        grid=(nb, s // t),
        in_specs=[pl.BlockSpec((None, t, d), lambda b, i: (b, i, 0)),
                  _const_spec((d, f)), _const_spec((d, f)), _const_spec((f, d)),
                  _const_spec((1, d)), _const_spec((1, d))],
        out_specs=pl.BlockSpec((None, t, d), lambda b, i: (b, i, 0)),
        out_shape=jax.ShapeDtypeStruct((nb, s, d), _F32),
        compiler_params=pltpu.CompilerParams(
            dimension_semantics=("parallel", "parallel"), vmem_limit_bytes=VMEM_LIMIT),
        name="ffn_dense",
    )(h, w_gate, w_up, w_down, ln_g, ln_b)


def _depthwise_conv(ubuf, wbc, cbuf, t, d, k):
    off0 = HALO - k // 2
    taps = {}
    for kk in range(k):
        a, b = divmod(off0 + kk, 8)
        taps.setdefault(b, []).append((a, kk))
    bs = sorted(taps)
    a_all = sorted({a for lst in taps.values() for a, _ in lst})
    assert t % 8 == 0 and (t + 2 * HALO) // 8 >= t // 8 + a_all[-1] + 1
    n_slab = t // 8
    group = min(CONV_SLAB_GROUP, n_slab)
    rows = lax.broadcasted_iota(jnp.int32, (8, LANES), 0)

    def slab_start(m):
        return m * 8 if isinstance(m, int) else pl.multiple_of(m * 8, 8)

    def tree_sum(vals):
        while len(vals) > 1:
            vals = [vals[i] + vals[i + 1] for i in range(0, len(vals) - 1, 2)] + vals[len(vals) & ~1:]
        return vals[0]

    for c in range(d // LANES):
        lanes = pl.ds(c * LANES, LANES)

        def zslab(m, lanes=lanes):
            u = {a: ubuf[pl.ds(slab_start(m + a), 8), lanes] for a in a_all}
            return tuple(tree_sum([wbc[kk * 8:(kk + 1) * 8, lanes] * u[a] for a, kk in taps[b]])
                         for b in bs)

        def combine(zp, zn):
            return tree_sum([p if b == 0 else pltpu.roll(jnp.where(rows < b, n, p), 8 - b, 0)
                             for b, p, n in zip(bs, zp, zn)])

        def slabs(j0, count, zp, lanes=lanes):
            for g in range(count):
                zn = zslab(j0 + g + 1)
                cbuf[pl.ds(slab_start(j0 + g), 8), lanes] = combine(zp, zn)
                zp = zn
            return zp

        n_loop = n_slab // group
        zp = lax.fori_loop(0, n_loop, lambda i, z: slabs(i * group, group, z), zslab(0))
        slabs(n_loop * group, n_slab - n_loop * group, zp)


def _mixer_b_kernel(hp_ref, hm_ref, hn_ref, wa_ref, wg_ref, b1_ref, cw_ref, cb_ref, lg_ref, lb_ref,
                    w2_ref, b2_ref, g_ref, b_ref, rt_ref, out_ref, route_ref, pos_ref, cnt_ref,
                    xbuf, ubuf, cbuf, wbc, pmat, *, alpha, n_experts, cap):
    t, d = hm_ref.shape
    tp = pmat.shape[0]
    _fill_xbuf(xbuf, hp_ref, hm_ref, hn_ref, t)
    xa = xbuf[...]
    a = _dot(xa, wa_ref[...]) + b1_ref[:, 0:d]
    gl = _dot(xa, wg_ref[...]) + b1_ref[:, d:2 * d]
    ubuf[...] = jnp.where(_halo_row_mask(t, d), a * jax.nn.sigmoid(gl), 0.0)
    k = cw_ref.shape[0]
    for kk in range(k):
        wbc[kk * 8:(kk + 1) * 8, :] = jnp.broadcast_to(cw_ref[kk:kk + 1, :], (8, d))
    _depthwise_conv(ubuf, wbc, cbuf, t, d, k)
    u = _silu(_layer_norm(cbuf[...] + cb_ref[...], lg_ref[...], lb_ref[...]))
    mix = _dot(u.astype(_BF16), w2_ref[...]) + b2_ref[...]
    h1 = _layer_norm(alpha * hm_ref[...] + mix, g_ref[...], b_ref[...])
    out_ref[...] = h1

    hi = h1.astype(_BF16)
    lo = (h1 - hi.astype(_F32)).astype(_BF16)
    p_hi = _dot(hi, rt_ref[...])
    logits = p_hi[:, 0:LANES] + p_hi[:, LANES:2 * LANES] + _dot(lo, rt_ref[:, 0:LANES])
    lane = lax.broadcasted_iota(jnp.int32, logits.shape, 1).astype(_F32)
    neg = jnp.float32(-jnp.inf)
    lg1 = jnp.where(lane < n_experts, logits, neg)
    m1 = jnp.max(lg1, axis=-1, keepdims=True)
    i1 = jnp.min(jnp.where(lg1 == m1, lane, float(LANES)), axis=-1, keepdims=True)
    lg2 = jnp.where(lane == i1, neg, lg1)
    m2 = jnp.max(lg2, axis=-1, keepdims=True)
    i2 = jnp.min(jnp.where(lg2 == m2, lane, float(LANES)), axis=-1, keepdims=True)
    e2 = jnp.exp(m2 - m1)
    den = 1.0 + e2
    route = jnp.where(lane == 0.0, i1, jnp.where(lane == 1.0, i2,
                      jnp.where(lane == 2.0, 1.0 / den, jnp.where(lane == 3.0, e2 / den, 0.0))))
    route_ref[...] = route

    @pl.when(jnp.logical_and(pl.program_id(0) == 0, pl.program_id(1) == 0))
    def _():
        cnt_ref[...] = jnp.zeros_like(cnt_ref)

    member = jnp.where(jnp.logical_or(lane == i1, lane == i2), 1.0, 0.0)
    if tp > t:
        pmat[t:tp, :] = jnp.zeros((tp - t, LANES), _F32)
    pmat[0:t, :] = member
    earlier = (lax.broadcasted_iota(jnp.int32, (t, tp), 1) < lax.broadcasted_iota(jnp.int32, (t, tp), 0))
    before = _dot(jnp.where(earlier, 1.0, 0.0).astype(_BF16), pmat[...].astype(_BF16))
    slot = before + cnt_ref[...] + lane * float(cap)
    pos1 = jnp.sum(jnp.where(lane == i1, slot, 0.0), axis=-1, keepdims=True)
    pos2 = jnp.sum(jnp.where(lane == i2, slot, 0.0), axis=-1, keepdims=True)
    cnt_ref[...] += jnp.sum(member, axis=0, keepdims=True)
    pmat[0:t, :] = jnp.where(lane == 0.0, pos1, jnp.where(lane == 1.0, pos2, 0.0))
    pos_ref[...] = pmat[...].T[0:8, :].astype(jnp.int32)


def _mixer_b(h, w_pw1, b_pw1, conv_w, conv_b, cln_g, cln_b, w_pw2, b_pw2, ln_g, ln_b, router_pad,
             alpha, n_experts, cap):
    nb, s, d = h.shape
    t = _row_tile(s)
    tp = -(-t // LANES) * LANES
    prev, main, nxt = _seq_specs(t, d, s // HALO)
    k = conv_w.shape[0]
    w_spec = lambda j: pl.BlockSpec((d, d), lambda b, i, j=j: (0, j), pipeline_mode=pl.Buffered(1))
    vec = _const_spec((1, d))
    return pl.pallas_call(
        functools.partial(_mixer_b_kernel, alpha=alpha, n_experts=n_experts, cap=cap),
        grid=(nb, s // t),
        in_specs=[prev, main, nxt, w_spec(0), w_spec(1), _const_spec((1, 2 * d)),
                  _const_spec((k, d)), vec, vec, vec, _const_spec((d, d)), vec, vec, vec,
                  _const_spec((d, 2 * LANES))],
        out_specs=[pl.BlockSpec((None, t, d), lambda b, i: (b, i, 0)),
                   pl.BlockSpec((None, t, LANES), lambda b, i: (b, i, 0)),
                   pl.BlockSpec((None, None, 8, tp), lambda b, i: (b, i, 0, 0)),
                   pl.BlockSpec((1, LANES), lambda b, i: (0, 0))],
        out_shape=[jax.ShapeDtypeStruct((nb, s, d), _F32),
                   jax.ShapeDtypeStruct((nb, s, LANES), _F32),
                   jax.ShapeDtypeStruct((nb, s // t, 8, tp), jnp.int32),
                   jax.ShapeDtypeStruct((1, LANES), _F32)],
        scratch_shapes=[pltpu.VMEM((t + 2 * HALO, d), _BF16), pltpu.VMEM((t + 2 * HALO, d), _F32),
                        pltpu.VMEM((t, d), _F32), pltpu.VMEM((8 * k, d), _F32),
                        pltpu.VMEM((tp, LANES), _F32)],
        compiler_params=pltpu.CompilerParams(
            dimension_semantics=("arbitrary", "arbitrary"), vmem_limit_bytes=VMEM_LIMIT),
        name="mixer_b",
    )(h, h, h, w_pw1, w_pw1, b_pw1, conv_w, conv_b, cln_g, cln_b, w_pw2, b_pw2, ln_g, ln_b,
      router_pad)


ROWS_PER_ISSUE = 8


def _rows_to_tiles(dst_ref, x):
    n, d = x.shape
    c = d // LANES
    for j in range(c):
        dst_ref[pl.ds(j, n, stride=c), :] = x[:, j * LANES:(j + 1) * LANES]


def _tiles_to_rows(src_ref, n, c):
    return jnp.concatenate([src_ref[pl.ds(j, n, stride=c), :] for j in range(c)], axis=1)


def _token_tile(ref, token, c):
    start = token * c
    if c % 8 == 0:
        start = pl.multiple_of(start, 8)
    return ref.at[pl.ds(start, c), :]


def _dispatch_kernel(cnt_ref, pos0_ref, pos1_ref, h_ref, xs_hbm, tbuf, zbuf, sem, zsem,
                     *, cap, tm, n_experts):
    i = pl.program_id(0)
    last = pl.num_programs(0) - 1
    slot = i % 2
    t, d = h_ref.shape
    c = d // LANES

    def zero_block(e):
        start = (e * cap + cnt_ref[e]) * c
        if c % 8 == 0:
            start = pl.multiple_of(start, 8)
        return pltpu.make_async_copy(zbuf, xs_hbm.at[pl.ds(start, tm * c), :], zsem)

    def wait_step(s):
        for _ in range(2):
            pltpu.make_async_copy(tbuf.at[s], tbuf.at[s], sem.at[s]).wait()

    @pl.when(i == 0)
    def _():
        zbuf[...] = jnp.zeros_like(zbuf)
        for e in range(n_experts):
            zero_block(e).start()

    @pl.when(i >= 2)
    def _():
        wait_step(slot)

    _rows_to_tiles(tbuf.at[slot], h_ref[...])

    def body(q, carry):
        for u in range(ROWS_PER_ISSUE):
            r = q * ROWS_PER_ISSUE + u
            for p_ref in (pos0_ref, pos1_ref):
                pltpu.make_async_copy(_token_tile(tbuf.at[slot], r, c),
                                      _token_tile(xs_hbm, p_ref[0, 0, r], c), sem.at[slot]).start()
        return carry
    lax.fori_loop(0, t // ROWS_PER_ISSUE, body, 0)

    @pl.when(i == last)
    def _():
        wait_step(slot)

        @pl.when(i >= 1)
        def _():
            wait_step(1 - slot)
        for e in range(n_experts):
            zero_block(e).wait()


def _dispatch(h_rows, picks, counts, t, cap, tm, n_experts):
    n, d = h_rows.shape
    c = d // LANES
    n_tiles, _, tp = picks[0].shape
    assert n_tiles * t == n and t % ROWS_PER_ISSUE == 0 and c * LANES == d
    pick_spec = pl.BlockSpec((1, 1, tp), lambda i, cnt: (i, 0, 0), memory_space=pltpu.SMEM)
    grid_spec = pltpu.PrefetchScalarGridSpec(
        num_scalar_prefetch=1,
        grid=(n_tiles,),
        in_specs=[pick_spec, pick_spec, pl.BlockSpec((t, d), lambda i, cnt: (i, 0))],
        out_specs=pl.BlockSpec(memory_space=pl.ANY),
        scratch_shapes=[pltpu.VMEM((2, t * c, LANES), _F32), pltpu.VMEM((tm * c, LANES), _F32),
                        pltpu.SemaphoreType.DMA((2,)), pltpu.SemaphoreType.DMA(())],
    )
    return pl.pallas_call(
        functools.partial(_dispatch_kernel, cap=cap, tm=tm, n_experts=n_experts),
        grid_spec=grid_spec,
        out_shape=jax.ShapeDtypeStruct((n_experts * cap * c, LANES), _F32),
        compiler_params=pltpu.CompilerParams(
            dimension_semantics=("arbitrary",), vmem_limit_bytes=VMEM_LIMIT),
        name="moe_dispatch",
    )(counts, picks[0], picks[1], h_rows)


def _moe_kernel(texp_ref, tblk_ref, nact_ref, x_ref, wg_ref, wu_ref, wd_ref, y_ref, *, f_chunk):
    del texp_ref, tblk_ref
    c = wg_ref.shape[0] // LANES
    tm = x_ref.shape[0] // c

    @pl.when(pl.program_id(0) < nact_ref[0])
    def _():
        x = _tiles_to_rows(x_ref, tm, c).astype(_BF16)
        acc = None
        for s, n in _chunks(wg_ref.shape[1], f_chunk):
            gate = _dot(x, wg_ref[:, s:s + n])
            up = _dot(x, wu_ref[:, s:s + n])
            part = _dot((_silu(gate) * up).astype(_BF16), wd_ref[s:s + n, :])
            acc = part if acc is None else acc + part
        _rows_to_tiles(y_ref, acc)


def _tile_plan(counts, n_experts, tm, cap, m_tiles):
    tiles = (counts + tm - 1) // tm
    ends = jnp.cumsum(tiles)
    n_act = ends[-1]
    m = jnp.minimum(jnp.arange(m_tiles, dtype=jnp.int32), n_act - 1)
    te = jnp.minimum(jnp.sum((m[:, None] >= ends[None, :]).astype(jnp.int32), axis=1), n_experts - 1)
    first = jnp.sum(jnp.where(te[:, None] == jnp.arange(n_experts, dtype=jnp.int32)[None, :],
                              (ends - tiles)[None, :], 0), axis=1)
    tb = te * (cap // tm) + (m - first)
    return te.astype(jnp.int32), tb.astype(jnp.int32), n_act.reshape(1).astype(jnp.int32)


def _moe_ffn(xs, plan, w_gate, w_up, w_down, tm, f_chunk):
    _, d, fe = w_gate.shape
    c = d // LANES
    te, tb, n_act = plan
    m_tiles = te.shape[0]
    w_spec = lambda shape: pl.BlockSpec((None,) + shape, lambda m, te, tb, na: (te[m], 0, 0),
                                        pipeline_mode=pl.Buffered(1))
    grid_spec = pltpu.PrefetchScalarGridSpec(
        num_scalar_prefetch=3,
        grid=(m_tiles,),
        in_specs=[pl.BlockSpec((tm * c, LANES), lambda m, te, tb, na: (tb[m], 0)),
                  w_spec((d, fe)), w_spec((d, fe)), w_spec((fe, d))],
        out_specs=pl.BlockSpec((tm * c, LANES), lambda m, te, tb, na: (tb[m], 0)),
    )
    return pl.pallas_call(
        functools.partial(_moe_kernel, f_chunk=f_chunk),
        grid_spec=grid_spec,
        out_shape=jax.ShapeDtypeStruct(xs.shape, _F32),
        compiler_params=pltpu.CompilerParams(
            dimension_semantics=("arbitrary",), vmem_limit_bytes=VMEM_LIMIT),
        name="moe_ffn",
    )(te, tb, n_act, xs, w_gate, w_up, w_down)


def _combine_kernel(pos0_ref, pos1_ref, pos0n_ref, pos1n_ref, h_ref, route_ref, g_ref, b_ref, y_hbm,
                    out_ref, gbuf, sem, *scratch, alpha, tiles_per_seq):
    i = pl.program_id(0)
    n = pl.num_programs(0)
    t, d = h_ref.shape
    c = d // LANES
    slot = i % 2
    final = tiles_per_seq is not None
    chunk = min(ROWS_PER_ISSUE, t)

    def issue(p_refs, s, r0, rows):
        for u in range(rows):
            for k, p_ref in enumerate(p_refs):
                pltpu.make_async_copy(_token_tile(y_hbm, p_ref[0, 0, r0 + u], c),
                                      _token_tile(gbuf.at[s, k], r0 + u, c), sem.at[s]).start()

    def loop(fn):
        n_loop = t // chunk

        def body(q, carry):
            fn(pl.multiple_of(q * chunk, 8), chunk)
            return carry
        lax.fori_loop(0, n_loop, body, 0)
        if t > n_loop * chunk:
            fn(n_loop * chunk, t - n_loop * chunk)

    if final:
        obuf, sem_head, sem_body = scratch

        def writeback(step, s):
            b, j = step // tiles_per_seq, step % tiles_per_seq
            head = pltpu.make_async_copy(
                obuf.at[s, pl.ds(0, N_META), :],
                out_ref.at[b, pl.ds(pl.multiple_of(jnp.maximum(j * t - N_META, 0), 8), N_META), :],
                sem_head.at[s])
            body = pltpu.make_async_copy(
                obuf.at[s, pl.ds(N_META, t - N_META), :],
                out_ref.at[b, pl.ds(pl.multiple_of(j * t, 8), t - N_META), :], sem_body.at[s])
            return j > 0, head, body

        def wait_writeback(step, s):
            has_head, head, body = writeback(step, s)
            body.wait()

            @pl.when(has_head)
            def _():
                head.wait()

        dst = obuf.at[slot]
    else:
        dst = out_ref

    @pl.when(i == 0)
    def _():
        loop(lambda r0, rows: issue((pos0_ref, pos1_ref), 0, r0, rows))

    @pl.when(i + 1 < n)
    def _():
        loop(lambda r0, rows: issue((pos0n_ref, pos1n_ref), 1 - slot, r0, rows))

    pltpu.make_async_copy(gbuf.at[slot], gbuf.at[slot], sem.at[slot]).wait()
    if final:
        @pl.when(i >= 2)
        def _():
            wait_writeback(i - 2, slot)

    def compute(r0, rows):
        sel = pl.ds(r0, rows)
        base = r0 * c
        if c % 8 == 0 and not isinstance(base, int):
            base = pl.multiple_of(base, 8)
        y = [jnp.concatenate([gbuf[slot, k, pl.ds(base + j, rows, stride=c), :] for j in range(c)], axis=1)
             for k in range(2)]
        ff = route_ref[sel, 2:3] * y[0] + route_ref[sel, 3:4] * y[1]
        dst[sel, :] = _layer_norm(alpha * h_ref[sel, :] + ff, g_ref[...], b_ref[...])

    compute(0, t)

    if final:
        has_head, head, body = writeback(i, slot)
        body.start()

        @pl.when(has_head)
        def _():
            head.start()

        @pl.when(i == n - 1)
        def _():
            wait_writeback(i, slot)

            @pl.when(i >= 1)
            def _():
                wait_writeback(i - 1, 1 - slot)


def _combine(h_rows, ys, route, picks, ln_g, ln_b, alpha, t, final_shape=None):
    n, d = h_rows.shape
    n_tiles, _, tp = picks[0].shape
    assert n_tiles * t == n and t % ROWS_PER_ISSUE == 0 and t > N_META
    pick_spec = lambda fn: pl.BlockSpec((1, 1, tp), fn, memory_space=pltpu.SMEM)
    this_tile = pick_spec(lambda i: (i, 0, 0))
    next_tile = pick_spec(lambda i: (jnp.minimum(i + 1, n_tiles - 1), 0, 0))
    scratch = [pltpu.VMEM((2, 2, t * (d // LANES), LANES), _F32), pltpu.SemaphoreType.DMA((2,))]
    if final_shape is None:
        tiles_per_seq = None
        out_specs = pl.BlockSpec((t, d), lambda i: (i, 0))
        out_shape = jax.ShapeDtypeStruct((n, d), _F32)
    else:
        nb, seq = final_shape
        tiles_per_seq = (seq + N_META) // t
        assert nb * tiles_per_seq == n_tiles
        out_specs = pl.BlockSpec(memory_space=pl.ANY)
        out_shape = jax.ShapeDtypeStruct((nb, seq, d), _F32)
        scratch += [pltpu.VMEM((2, t, d), _F32), pltpu.SemaphoreType.DMA((2,)),
                    pltpu.SemaphoreType.DMA((2,))]
    return pl.pallas_call(
        functools.partial(_combine_kernel, alpha=alpha, tiles_per_seq=tiles_per_seq),
        grid=(n_tiles,),
        in_specs=[this_tile, this_tile, next_tile, next_tile,
                  pl.BlockSpec((t, d), lambda i: (i, 0)),
                  pl.BlockSpec((t, LANES), lambda i: (i, 0)),
                  _const_spec((1, d)), _const_spec((1, d)),
                  pl.BlockSpec(memory_space=pl.ANY)],
        out_specs=out_specs,
        out_shape=out_shape,
        scratch_shapes=scratch,
        compiler_params=pltpu.CompilerParams(
            dimension_semantics=("arbitrary",), vmem_limit_bytes=VMEM_LIMIT),
        name="moe_combine",
    )(picks[0], picks[1], picks[0], picks[1], h_rows, route, ln_g, ln_b, ys)


def _encode(x, p, moe_tile):
    nb, seq, d = x.shape
    depth = p["ln_mix_g"].shape[0]
    alpha = (2.0 * depth) ** 0.25
    n_experts = p["moe_router"].shape[-1]
    meta = jnp.broadcast_to(p["meta_tokens"].astype(x.dtype)[None], (nb, N_META, d))
    h = lax.dynamic_update_slice(jnp.pad(x, ((0, 0), (N_META, 0), (0, 0))), meta, (0, 0, 0))
    s = h.shape[1]
    row = lambda v: v.reshape(1, -1)
    for i in range(depth):
        j = i // 2
        if i % 2 == 0:
            h = _mixer_a(h, p["a_w_in"][j], p["a_conv_w"][j], p["a_w_out"][j],
                         row(p["ln_mix_g"][i]), row(p["ln_mix_b"][i]), alpha)
            h = _ffn(h, p["ffn_w_gate"][j], p["ffn_w_up"][j], p["ffn_w_down"][j],
                     row(p["ln_ffn_g"][i]), row(p["ln_ffn_b"][i]), alpha)
        else:
            router_f32 = jnp.pad(p["moe_router"][j].astype(_F32), ((0, 0), (0, LANES - n_experts)))
            router_hi = router_f32.astype(_BF16)
            router_lo = (router_f32 - router_hi.astype(_F32)).astype(_BF16)
            router_pad = jnp.concatenate([router_hi, router_lo], axis=1)
            n = nb * s
            t = _row_tile(s)
            cap = -(-n // moe_tile) * moe_tile + moe_tile
            h1, route, pos, cnt = _mixer_b(h, p["b_w_pw1"][j], row(p["b_b_pw1"][j]), p["b_conv_w"][j],
                                           row(p["b_conv_b"][j]), row(p["b_ln_g"][j]),
                                           row(p["b_ln_b"][j]), p["b_w_pw2"][j], row(p["b_b_pw2"][j]),
                                           row(p["ln_mix_g"][i]), row(p["ln_mix_b"][i]), router_pad,
                                           alpha, n_experts, cap)
            h1r = h1.reshape(n, d)
            pos = pos.reshape(n // t, 8, pos.shape[-1])
            picks = (pos[:, 0:1, :], pos[:, 1:2, :])
            counts = cnt[0, :n_experts].astype(jnp.int32)
            xs = _dispatch(h1r, picks, counts, t, cap, moe_tile, n_experts)
            plan = _tile_plan(counts, n_experts, moe_tile, cap, (2 * n) // moe_tile + n_experts)
            fe = p["moe_w_gate"].shape[-1]
            ys = _moe_ffn(xs, plan, p["moe_w_gate"][j], p["moe_w_up"][j], p["moe_w_down"][j],
                          moe_tile, fe // 2)
            last = i == depth - 1
            h = _combine(h1r, ys, route.reshape(n, LANES), picks, row(p["ln_ffn_g"][i]),
                         row(p["ln_ffn_b"][i]), alpha, t, (nb, seq) if last else None)
            if last:
                return h
            h = h.reshape(nb, s, d)
    return h[:, N_META:]


_MATMUL_WEIGHTS = ("a_w_in", "a_w_out", "b_w_pw1", "b_w_pw2", "ffn_w_gate", "ffn_w_up", "ffn_w_down",
                   "moe_w_gate", "moe_w_up", "moe_w_down")


def kernel(x_prompt, x_sample, meta_tokens, a_w_in, a_conv_w, a_w_out, b_w_pw1, b_b_pw1, b_conv_w,
           b_conv_b, b_ln_g, b_ln_b, b_w_pw2, b_b_pw2, ffn_w_gate, ffn_w_up, ffn_w_down, moe_router,
           moe_w_gate, moe_w_up, moe_w_down, ln_mix_g, ln_mix_b, ln_ffn_g, ln_ffn_b):
    p = dict(meta_tokens=meta_tokens, a_w_in=a_w_in, a_conv_w=a_conv_w, a_w_out=a_w_out,
             b_w_pw1=b_w_pw1, b_b_pw1=b_b_pw1, b_conv_w=b_conv_w, b_conv_b=b_conv_b, b_ln_g=b_ln_g,
             b_ln_b=b_ln_b, b_w_pw2=b_w_pw2, b_b_pw2=b_b_pw2, ffn_w_gate=ffn_w_gate,
             ffn_w_up=ffn_w_up, ffn_w_down=ffn_w_down, moe_router=moe_router, moe_w_gate=moe_w_gate,
             moe_w_up=moe_w_up, moe_w_down=moe_w_down, ln_mix_g=ln_mix_g, ln_mix_b=ln_mix_b,
             ln_ffn_g=ln_ffn_g, ln_ffn_b=ln_ffn_b)
    for name in _MATMUL_WEIGHTS:
        p[name] = p[name].astype(_BF16)
    return (_encode(x_prompt, p, MOE_ROW_TILE), _encode(x_sample, p, MOE_ROW_TILE))
```

```python
import functools

import jax
import jax.numpy as jnp
from jax import lax
from jax.experimental import pallas as pl
from jax.experimental.pallas import tpu as pltpu

N_META = 16
LN_EPS = 1e-5
HALO = 16
LANES = 128
BF16_ROWS = 16
MAX_ROW_TILE = 1024
MOE_ROW_TILE = 512
CONV_SLAB_GROUP = 6
VMEM_LIMIT = 56 * 1024 * 1024

_BF16 = jnp.bfloat16
_F32 = jnp.float32


def _row_tile(rows):
    best = None
    for t in range(BF16_ROWS, MAX_ROW_TILE + 1, BF16_ROWS):
        if rows % t == 0:
            best = t
    assert best is not None, rows
    return best


def _chunks(total, size):
    out, s = [], 0
    while s < total:
        n = min(size, total - s)
        out.append((s, n))
        s += n
    return out


def _dot(a, b):
    return jnp.dot(a, b, preferred_element_type=_F32)


def _layer_norm(x, g, b):
    mu = jnp.mean(x, axis=-1, keepdims=True)
    xc = x - mu
    var = jnp.mean(xc * xc, axis=-1, keepdims=True)
    return xc * lax.rsqrt(var + LN_EPS) * g + b


def _silu(x):
    return x * jax.nn.sigmoid(x)


def _const_spec(shape):
    nd = len(shape)
    return pl.BlockSpec(shape, lambda *_: (0,) * nd, pipeline_mode=pl.Buffered(1))


def _layer_spec(layer, shape, col=0):
    return pl.BlockSpec((None,) + shape, lambda *_: (layer, 0, col), pipeline_mode=pl.Buffered(1))


def _seq_specs(t, d, n_halo_blocks):
    per_tile = t // HALO
    main = pl.BlockSpec((None, t, d), lambda b, i: (b, i, 0))
    prev = pl.BlockSpec((None, HALO, d), lambda b, i: (b, jnp.maximum(i * per_tile - 1, 0), 0))
    nxt = pl.BlockSpec((None, HALO, d),
                       lambda b, i: (b, jnp.minimum((i + 1) * per_tile, n_halo_blocks - 1), 0))
    return prev, main, nxt


def _fill_xbuf(xbuf, hp_ref, hm_ref, hn_ref, t):
    xbuf[0:HALO, :] = hp_ref[...].astype(_BF16)
    xbuf[HALO:HALO + t, :] = hm_ref[...].astype(_BF16)
    xbuf[HALO + t:HALO + t + HALO, :] = hn_ref[...].astype(_BF16)


def _halo_row_mask(t, d):
    i = pl.program_id(1)
    last = pl.num_programs(1) - 1
    r = lax.broadcasted_iota(jnp.int32, (t + 2 * HALO, d), 0)
    return jnp.logical_and(jnp.logical_or(r >= HALO, i > 0),
                           jnp.logical_or(r < HALO + t, i < last))


def _mixer_a_kernel(hp_ref, hm_ref, hn_ref, wb_ref, wc_ref, wh_ref, cw_ref, wo_ref,
                    g_ref, b_ref, out_ref, xbuf, pbuf, *, alpha):
    t, d = hm_ref.shape
    _fill_xbuf(xbuf, hp_ref, hm_ref, hn_ref, t)
    xa = xbuf[...]
    p = _dot(xa, wc_ref[...]) * _dot(xa, wh_ref[...])
    pbuf[...] = jnp.where(_halo_row_mask(t, d), p, 0.0)
    k = cw_ref.shape[0]
    conv = None
    for j in range(k):
        s = HALO - k // 2 + j
        term = cw_ref[j:j + 1, :] * pbuf[s:s + t, :]
        conv = term if conv is None else conv + term
    gate = _dot(xbuf[HALO:HALO + t, :], wb_ref[...])
    mix = _dot((gate * conv).astype(_BF16), wo_ref[...])
    out_ref[...] = _layer_norm(alpha * hm_ref[...] + mix, g_ref[...], b_ref[...])


def _mixer_a(h, w_in, conv_w, w_out, ln_g, ln_b, alpha, layer):
    nb, s, d = h.shape
    t = _row_tile(s)
    prev, main, nxt = _seq_specs(t, d, s // HALO)
    k = conv_w.shape[0]
    w_spec = lambda j: _layer_spec(layer, (d, d), j)
    return pl.pallas_call(
        functools.partial(_mixer_a_kernel, alpha=alpha),
        grid=(nb, s // t),
        in_specs=[prev, main, nxt, w_spec(0), w_spec(1), w_spec(2), _const_spec((k, d)),
                  w_spec(0), _const_spec((1, d)), _const_spec((1, d))],
        out_specs=pl.BlockSpec((None, t, d), lambda b, i: (b, i, 0)),
        out_shape=jax.ShapeDtypeStruct((nb, s, d), _F32),
        scratch_shapes=[pltpu.VMEM((t + 2 * HALO, d), _BF16), pltpu.VMEM((t + 2 * HALO, d), _F32)],
        compiler_params=pltpu.CompilerParams(
            dimension_semantics=("parallel", "parallel"), vmem_limit_bytes=VMEM_LIMIT),
        name="mixer_a",
    )(h, h, h, w_in, w_in, w_in, conv_w, w_out, ln_g, ln_b)


def _ffn_kernel(h_ref, wg_ref, wu_ref, wd_ref, g_ref, b_ref, out_ref, *, alpha, f_chunk):
    h = h_ref[...]
    x = h.astype(_BF16)
    acc = None
    for s, n in _chunks(wg_ref.shape[1], f_chunk):
        gate = _dot(x, wg_ref[:, s:s + n])
        up = _dot(x, wu_ref[:, s:s + n])
        part = _dot((_silu(gate) * up).astype(_BF16), wd_ref[s:s + n, :])
        acc = part if acc is None else acc + part
    out_ref[...] = _layer_norm(alpha * h + acc, g_ref[...], b_ref[...])


def _ffn(h, w_gate, w_up, w_down, ln_g, ln_b, alpha, layer):
    nb, s, d = h.shape
    f = w_gate.shape[2]
    t = _row_tile(s)
    return pl.pallas_call(
        functools.partial(_ffn_kernel, alpha=alpha, f_chunk=1024),
        grid=(nb, s // t),
        in_specs=[pl.BlockSpec((None, t, d), lambda b, i: (b, i, 0)),
                  _layer_spec(layer, (d, f)), _layer_spec(layer, (d, f)), _layer_spec(layer, (f, d)),
                  _const_spec((1, d)), _const_spec((1, d))],
        out_specs=pl.BlockSpec((None, t, d), lambda b, i: (b, i, 0)),
        out_shape=jax.ShapeDtypeStruct((nb, s, d), _F32),
        compiler_params=pltpu.CompilerParams(
            dimension_semantics=("parallel", "parallel"), vmem_limit_bytes=VMEM_LIMIT),
        name="ffn_dense",
    )(h, w_gate, w_up, w_down, ln_g, ln_b)


def _depthwise_conv(ubuf, wbc, cbuf, t, d, k):
    off0 = HALO - k // 2
    taps = {}
    for kk in range(k):
        a, b = divmod(off0 + kk, 8)
        taps.setdefault(b, []).append((a, kk))
    bs = sorted(taps)
    a_all = sorted({a for lst in taps.values() for a, _ in lst})
    assert t % 8 == 0 and (t + 2 * HALO) // 8 >= t // 8 + a_all[-1] + 1
    n_slab = t // 8
    group = min(CONV_SLAB_GROUP, n_slab)
    rows = lax.broadcasted_iota(jnp.int32, (8, LANES), 0)

    def slab_start(m):
        return m * 8 if isinstance(m, int) else pl.multiple_of(m * 8, 8)

    def tree_sum(vals):
        while len(vals) > 1:
            vals = [vals[i] + vals[i + 1] for i in range(0, len(vals) - 1, 2)] + vals[len(vals) & ~1:]
        return vals[0]

    for c in range(d // LANES):
        lanes = pl.ds(c * LANES, LANES)

        def zslab(m, lanes=lanes):
            u = {a: ubuf[pl.ds(slab_start(m + a), 8), lanes] for a in a_all}
            return tuple(tree_sum([wbc[kk * 8:(kk + 1) * 8, lanes] * u[a] for a, kk in taps[b]])
                         for b in bs)

        def combine(zp, zn):
            return tree_sum([p if b == 0 else pltpu.roll(jnp.where(rows < b, n, p), 8 - b, 0)
                             for b, p, n in zip(bs, zp, zn)])

        def slabs(j0, count, zp, lanes=lanes):
            for g in range(count):
                zn = zslab(j0 + g + 1)
                cbuf[pl.ds(slab_start(j0 + g), 8), lanes] = combine(zp, zn)
                zp = zn
            return zp

        n_loop = n_slab // group
        zp = lax.fori_loop(0, n_loop, lambda i, z: slabs(i * group, group, z), zslab(0))
        slabs(n_loop * group, n_slab - n_loop * group, zp)


def _mixer_b_kernel(hp_ref, hm_ref, hn_ref, wa_ref, wg_ref, b1_ref, cw_ref, cb_ref, lg_ref, lb_ref,
                    w2_ref, b2_ref, g_ref, b_ref, rt_ref, out_ref, route_ref, pos_ref, cnt_ref,
                    xbuf, ubuf, cbuf, wbc, pmat, *, alpha, n_experts, cap):
    t, d = hm_ref.shape
    tp = pmat.shape[0]
    _fill_xbuf(xbuf, hp_ref, hm_ref, hn_ref, t)
    xa = xbuf[...]
    a = _dot(xa, wa_ref[...]) + b1_ref[:, 0:d]
    gl = _dot(xa, wg_ref[...]) + b1_ref[:, d:2 * d]
    ubuf[...] = jnp.where(_halo_row_mask(t, d), a * jax.nn.sigmoid(gl), 0.0)
    k = cw_ref.shape[0]
    for kk in range(k):
        wbc[kk * 8:(kk + 1) * 8, :] = jnp.broadcast_to(cw_ref[kk:kk + 1, :], (8, d))
    _depthwise_conv(ubuf, wbc, cbuf, t, d, k)
    u = _silu(_layer_norm(cbuf[...] + cb_ref[...], lg_ref[...], lb_ref[...]))
    mix = _dot(u.astype(_BF16), w2_ref[...]) + b2_ref[...]
    h1 = _layer_norm(alpha * hm_ref[...] + mix, g_ref[...], b_ref[...])
    out_ref[...] = h1

    hi = h1.astype(_BF16)
    lo = (h1 - hi.astype(_F32)).astype(_BF16)
    p_hi = _dot(hi, rt_ref[...])
    logits = p_hi[:, 0:LANES] + p_hi[:, LANES:2 * LANES] + _dot(lo, rt_ref[:, 0:LANES])
    lane = lax.broadcasted_iota(jnp.int32, logits.shape, 1).astype(_F32)
    neg = jnp.float32(-jnp.inf)
    lg1 = jnp.where(lane < n_experts, logits, neg)
    m1 = jnp.max(lg1, axis=-1, keepdims=True)
    i1 = jnp.min(jnp.where(lg1 == m1, lane, float(LANES)), axis=-1, keepdims=True)
    lg2 = jnp.where(lane == i1, neg, lg1)
    m2 = jnp.max(lg2, axis=-1, keepdims=True)
    i2 = jnp.min(jnp.where(lg2 == m2, lane, float(LANES)), axis=-1, keepdims=True)
    e2 = jnp.exp(m2 - m1)
    den = 1.0 + e2
    route = jnp.where(lane == 0.0, i1, jnp.where(lane == 1.0, i2,
                      jnp.where(lane == 2.0, 1.0 / den, jnp.where(lane == 3.0, e2 / den, 0.0))))
    route_ref[...] = route

    @pl.when(jnp.logical_and(pl.program_id(0) == 0, pl.program_id(1) == 0))
    def _():
        cnt_ref[...] = jnp.zeros_like(cnt_ref)

    member = jnp.where(jnp.logical_or(lane == i1, lane == i2), 1.0, 0.0)
    if tp > t:
        pmat[t:tp, :] = jnp.zeros((tp - t, LANES), _F32)
    pmat[0:t, :] = member
    earlier = (lax.broadcasted_iota(jnp.int32, (t, tp), 1) < lax.broadcasted_iota(jnp.int32, (t, tp), 0))
    before = _dot(jnp.where(earlier, 1.0, 0.0).astype(_BF16), pmat[...].astype(_BF16))
    slot = before + cnt_ref[...] + lane * float(cap)
    pos1 = jnp.sum(jnp.where(lane == i1, slot, 0.0), axis=-1, keepdims=True)
    pos2 = jnp.sum(jnp.where(lane == i2, slot, 0.0), axis=-1, keepdims=True)
    cnt_ref[...] += jnp.sum(member, axis=0, keepdims=True)
    pmat[0:t, :] = jnp.where(lane == 0.0, pos1, jnp.where(lane == 1.0, pos2, 0.0))
    pos_ref[...] = pmat[...].T[0:8, :].astype(jnp.int32)


def _mixer_b(h, w_pw1, b_pw1, conv_w, conv_b, cln_g, cln_b, w_pw2, b_pw2, ln_g, ln_b, router_pad,
             alpha, n_experts, cap, layer):
    nb, s, d = h.shape
    t = _row_tile(s)
    tp = -(-t // LANES) * LANES
    prev, main, nxt = _seq_specs(t, d, s // HALO)
    k = conv_w.shape[0]
    w_spec = lambda j: _layer_spec(layer, (d, d), j)
    vec = _const_spec((1, d))
    return pl.pallas_call(
        functools.partial(_mixer_b_kernel, alpha=alpha, n_experts=n_experts, cap=cap),
        grid=(nb, s // t),
        in_specs=[prev, main, nxt, w_spec(0), w_spec(1), _const_spec((1, 2 * d)),
                  _const_spec((k, d)), vec, vec, vec, w_spec(0), vec, vec, vec,
                  _const_spec((d, 2 * LANES))],
        out_specs=[pl.BlockSpec((None, t, d), lambda b, i: (b, i, 0)),
                   pl.BlockSpec((None, t, LANES), lambda b, i: (b, i, 0)),
                   pl.BlockSpec((None, None, 8, tp), lambda b, i: (b, i, 0, 0)),
                   pl.BlockSpec((1, LANES), lambda b, i: (0, 0))],
        out_shape=[jax.ShapeDtypeStruct((nb, s, d), _F32),
                   jax.ShapeDtypeStruct((nb, s, LANES), _F32),
                   jax.ShapeDtypeStruct((nb, s // t, 8, tp), jnp.int32),
                   jax.ShapeDtypeStruct((1, LANES), _F32)],
        scratch_shapes=[pltpu.VMEM((t + 2 * HALO, d), _BF16), pltpu.VMEM((t + 2 * HALO, d), _F32),
                        pltpu.VMEM((t, d), _F32), pltpu.VMEM((8 * k, d), _F32),
                        pltpu.VMEM((tp, LANES), _F32)],
        compiler_params=pltpu.CompilerParams(
            dimension_semantics=("arbitrary", "arbitrary"), vmem_limit_bytes=VMEM_LIMIT),
        name="mixer_b",
    )(h, h, h, w_pw1, w_pw1, b_pw1, conv_w, conv_b, cln_g, cln_b, w_pw2, b_pw2, ln_g, ln_b,
      router_pad)


ROWS_PER_ISSUE = 8


def _rows_to_tiles(dst_ref, x):
    n, d = x.shape
    c = d // LANES
    for j in range(c):
        dst_ref[pl.ds(j, n, stride=c), :] = x[:, j * LANES:(j + 1) * LANES]


def _tiles_to_rows(src_ref, n, c):
    return jnp.concatenate([src_ref[pl.ds(j, n, stride=c), :] for j in range(c)], axis=1)


def _token_tile(ref, token, c):
    start = token * c
    if c % 8 == 0:
        start = pl.multiple_of(start, 8)
    return ref.at[pl.ds(start, c), :]


def _dispatch_kernel(cnt_ref, pos0_ref, pos1_ref, h_ref, xs_hbm, tbuf, zbuf, sem, zsem,
                     *, cap, tm, n_experts):
    i = pl.program_id(0)
    last = pl.num_programs(0) - 1
    slot = i % 2
    t, d = h_ref.shape
    c = d // LANES

    def zero_block(e):
        start = (e * cap + cnt_ref[e]) * c
        if c % 8 == 0:
            start = pl.multiple_of(start, 8)
        return pltpu.make_async_copy(zbuf, xs_hbm.at[pl.ds(start, tm * c), :], zsem)

    def wait_step(s):
        for _ in range(2):
            pltpu.make_async_copy(tbuf.at[s], tbuf.at[s], sem.at[s]).wait()

    @pl.when(i == 0)
    def _():
        zbuf[...] = jnp.zeros_like(zbuf)
        for e in range(n_experts):
            zero_block(e).start()

    @pl.when(i >= 2)
    def _():
        wait_step(slot)

    _rows_to_tiles(tbuf.at[slot], h_ref[...])

    def body(q, carry):
        for u in range(ROWS_PER_ISSUE):
            r = q * ROWS_PER_ISSUE + u
            for k, p_ref in enumerate((pos0_ref, pos1_ref)):
                pltpu.make_async_copy(_token_tile(tbuf.at[slot], r, c),
                                      _token_tile(xs_hbm, p_ref[0, 0, r], c), sem.at[slot]).start(priority=k)
        return carry
    lax.fori_loop(0, t // ROWS_PER_ISSUE, body, 0)

    @pl.when(i == last)
    def _():
        wait_step(slot)

        @pl.when(i >= 1)
        def _():
            wait_step(1 - slot)
        for e in range(n_experts):
            zero_block(e).wait()


def _dispatch(h_rows, picks, counts, t, cap, tm, n_experts):
    n, d = h_rows.shape
    c = d // LANES
    n_tiles, _, tp = picks[0].shape
    assert n_tiles * t == n and t % ROWS_PER_ISSUE == 0 and c * LANES == d
    pick_spec = pl.BlockSpec((1, 1, tp), lambda i, cnt: (i, 0, 0), memory_space=pltpu.SMEM)
    grid_spec = pltpu.PrefetchScalarGridSpec(
        num_scalar_prefetch=1,
        grid=(n_tiles,),
        in_specs=[pick_spec, pick_spec, pl.BlockSpec((t, d), lambda i, cnt: (i, 0))],
        out_specs=pl.BlockSpec(memory_space=pl.ANY),
        scratch_shapes=[pltpu.VMEM((2, t * c, LANES), _F32), pltpu.VMEM((tm * c, LANES), _F32),
                        pltpu.SemaphoreType.DMA((2,)), pltpu.SemaphoreType.DMA(())],
    )
    return pl.pallas_call(
        functools.partial(_dispatch_kernel, cap=cap, tm=tm, n_experts=n_experts),
        grid_spec=grid_spec,
        out_shape=jax.ShapeDtypeStruct((n_experts * cap * c, LANES), _F32),
        compiler_params=pltpu.CompilerParams(
            dimension_semantics=("arbitrary",), vmem_limit_bytes=VMEM_LIMIT),
        name="moe_dispatch",
    )(counts, picks[0], picks[1], h_rows)


def _moe_kernel(texp_ref, tblk_ref, nact_ref, x_ref, wg_ref, wu_ref, wd_ref, y_ref, *, f_chunk):
    del texp_ref, tblk_ref
    c = wg_ref.shape[0] // LANES
    tm = x_ref.shape[0] // c

    @pl.when(pl.program_id(0) < nact_ref[0])
    def _():
        x = _tiles_to_rows(x_ref, tm, c).astype(_BF16)
        acc = None
        for s, n in _chunks(wg_ref.shape[1], f_chunk):
            gate = _dot(x, wg_ref[:, s:s + n])
            up = _dot(x, wu_ref[:, s:s + n])
            part = _dot((_silu(gate) * up).astype(_BF16), wd_ref[s:s + n, :])
            acc = part if acc is None else acc + part
        _rows_to_tiles(y_ref, acc)


def _tile_plan(counts, n_experts, tm, cap, m_tiles):
    tiles = (counts + tm - 1) // tm
    ends = jnp.cumsum(tiles)
    n_act = ends[-1]
    m = jnp.minimum(jnp.arange(m_tiles, dtype=jnp.int32), n_act - 1)
    te = jnp.minimum(jnp.sum((m[:, None] >= ends[None, :]).astype(jnp.int32), axis=1), n_experts - 1)
    first = jnp.sum(jnp.where(te[:, None] == jnp.arange(n_experts, dtype=jnp.int32)[None, :],
                              (ends - tiles)[None, :], 0), axis=1)
    tb = te * (cap // tm) + (m - first)
    return te.astype(jnp.int32), tb.astype(jnp.int32), n_act.reshape(1).astype(jnp.int32)


def _moe_ffn(xs, plan, w_gate, w_up, w_down, tm, f_chunk, layer):
    _, _, d, fe = w_gate.shape
    c = d // LANES
    te, tb, n_act = plan
    m_tiles = te.shape[0]
    w_spec = lambda shape: pl.BlockSpec((None, None) + shape, lambda m, te, tb, na: (layer, te[m], 0, 0),
                                        pipeline_mode=pl.Buffered(1))
    grid_spec = pltpu.PrefetchScalarGridSpec(
        num_scalar_prefetch=3,
        grid=(m_tiles,),
        in_specs=[pl.BlockSpec((tm * c, LANES), lambda m, te, tb, na: (tb[m], 0)),
                  w_spec((d, fe)), w_spec((d, fe)), w_spec((fe, d))],
        out_specs=pl.BlockSpec((tm * c, LANES), lambda m, te, tb, na: (tb[m], 0)),
    )
    return pl.pallas_call(
        functools.partial(_moe_kernel, f_chunk=f_chunk),
        grid_spec=grid_spec,
        out_shape=jax.ShapeDtypeStruct(xs.shape, _F32),
        compiler_params=pltpu.CompilerParams(
            dimension_semantics=("arbitrary",), vmem_limit_bytes=VMEM_LIMIT),
        name="moe_ffn",
    )(te, tb, n_act, xs, w_gate, w_up, w_down)


def _combine_kernel(pos0_ref, pos1_ref, pos0n_ref, pos1n_ref, h_ref, route_ref, g_ref, b_ref, y_hbm,
                    out_ref, gbuf, sem, *scratch, alpha, tiles_per_seq):
    i = pl.program_id(0)
    n = pl.num_programs(0)
    t, d = h_ref.shape
    c = d // LANES
    slot = i % 2
    final = tiles_per_seq is not None
    chunk = min(ROWS_PER_ISSUE, t)

    def issue(p_refs, s, r0, rows):
        for u in range(rows):
            for k, p_ref in enumerate(p_refs):
                pltpu.make_async_copy(_token_tile(y_hbm, p_ref[0, 0, r0 + u], c),
                                      _token_tile(gbuf.at[s, k], r0 + u, c), sem.at[s]).start(priority=k)

    def loop(fn):
        n_loop = t // chunk

        def body(q, carry):
            fn(pl.multiple_of(q * chunk, 8), chunk)
            return carry
        lax.fori_loop(0, n_loop, body, 0)
        if t > n_loop * chunk:
            fn(n_loop * chunk, t - n_loop * chunk)

    if final:
        obuf, sem_head, sem_body = scratch

        def writeback(step, s):
            b, j = step // tiles_per_seq, step % tiles_per_seq
            head = pltpu.make_async_copy(
                obuf.at[s, pl.ds(0, N_META), :],
                out_ref.at[b, pl.ds(pl.multiple_of(jnp.maximum(j * t - N_META, 0), 8), N_META), :],
                sem_head.at[s])
            body = pltpu.make_async_copy(
                obuf.at[s, pl.ds(N_META, t - N_META), :],
                out_ref.at[b, pl.ds(pl.multiple_of(j * t, 8), t - N_META), :], sem_body.at[s])
            return j > 0, head, body

        def wait_writeback(step, s):
            has_head, head, body = writeback(step, s)
            body.wait()

            @pl.when(has_head)
            def _():
                head.wait()

        dst = obuf.at[slot]
    else:
        dst = out_ref

    @pl.when(i == 0)
    def _():
        loop(lambda r0, rows: issue((pos0_ref, pos1_ref), 0, r0, rows))

    @pl.when(i + 1 < n)
    def _():
        loop(lambda r0, rows: issue((pos0n_ref, pos1n_ref), 1 - slot, r0, rows))

    pltpu.make_async_copy(gbuf.at[slot], gbuf.at[slot], sem.at[slot]).wait()
    if final:
        @pl.when(i >= 2)
        def _():
            wait_writeback(i - 2, slot)

    def compute(r0, rows):
        sel = pl.ds(r0, rows)
        base = r0 * c
        if c % 8 == 0 and not isinstance(base, int):
            base = pl.multiple_of(base, 8)
        y = [jnp.concatenate([gbuf[slot, k, pl.ds(base + j, rows, stride=c), :] for j in range(c)], axis=1)
             for k in range(2)]
        ff = route_ref[sel, 2:3] * y[0] + route_ref[sel, 3:4] * y[1]
        dst[sel, :] = _layer_norm(alpha * h_ref[sel, :] + ff, g_ref[...], b_ref[...])

    compute(0, t)

    if final:
        has_head, head, body = writeback(i, slot)
        body.start()

        @pl.when(has_head)
        def _():
            head.start()

        @pl.when(i == n - 1)
        def _():
            wait_writeback(i, slot)

            @pl.when(i >= 1)
            def _():
                wait_writeback(i - 1, 1 - slot)


def _combine(h_rows, ys, route, picks, ln_g, ln_b, alpha, t, final_shape=None):
    n, d = h_rows.shape
    n_tiles, _, tp = picks[0].shape
    assert n_tiles * t == n and t % ROWS_PER_ISSUE == 0 and t > N_META
    pick_spec = lambda fn: pl.BlockSpec((1, 1, tp), fn, memory_space=pltpu.SMEM)
    this_tile = pick_spec(lambda i: (i, 0, 0))
    next_tile = pick_spec(lambda i: (jnp.minimum(i + 1, n_tiles - 1), 0, 0))
    scratch = [pltpu.VMEM((2, 2, t * (d // LANES), LANES), _F32), pltpu.SemaphoreType.DMA((2,))]
    if final_shape is None:
        tiles_per_seq = None
        out_specs = pl.BlockSpec((t, d), lambda i: (i, 0))
        out_shape = jax.ShapeDtypeStruct((n, d), _F32)
    else:
        nb, seq = final_shape
        tiles_per_seq = (seq + N_META) // t
        assert nb * tiles_per_seq == n_tiles
        out_specs = pl.BlockSpec(memory_space=pl.ANY)
        out_shape = jax.ShapeDtypeStruct((nb, seq, d), _F32)
        scratch += [pltpu.VMEM((2, t, d), _F32), pltpu.SemaphoreType.DMA((2,)),
                    pltpu.SemaphoreType.DMA((2,))]
    return pl.pallas_call(
        functools.partial(_combine_kernel, alpha=alpha, tiles_per_seq=tiles_per_seq),
        grid=(n_tiles,),
        in_specs=[this_tile, this_tile, next_tile, next_tile,
                  pl.BlockSpec((t, d), lambda i: (i, 0)),
                  pl.BlockSpec((t, LANES), lambda i: (i, 0)),
                  _const_spec((1, d)), _const_spec((1, d)),
                  pl.BlockSpec(memory_space=pl.ANY)],
        out_specs=out_specs,
        out_shape=out_shape,
        scratch_shapes=scratch,
        compiler_params=pltpu.CompilerParams(
            dimension_semantics=("arbitrary",), vmem_limit_bytes=VMEM_LIMIT),
        name="moe_combine",
    )(picks[0], picks[1], picks[0], picks[1], h_rows, route, ln_g, ln_b, ys)


def _encode(x, p, moe_tile):
    nb, seq, d = x.shape
    depth = p["ln_mix_g"].shape[0]
    alpha = (2.0 * depth) ** 0.25
    n_experts = p["moe_router"].shape[-1]
    meta = jnp.broadcast_to(p["meta_tokens"].astype(x.dtype)[None], (nb, N_META, d))
    h = lax.dynamic_update_slice(jnp.pad(x, ((0, 0), (N_META, 0), (0, 0))), meta, (0, 0, 0))
    s = h.shape[1]
    row = lambda v: v.reshape(1, -1)
    for i in range(depth):
        j = i // 2
        if i % 2 == 0:
            h = _mixer_a(h, p["a_w_in"], p["a_conv_w"][j], p["a_w_out"],
                         row(p["ln_mix_g"][i]), row(p["ln_mix_b"][i]), alpha, j)
            h = _ffn(h, p["ffn_w_gate"], p["ffn_w_up"], p["ffn_w_down"],
                     row(p["ln_ffn_g"][i]), row(p["ln_ffn_b"][i]), alpha, j)
        else:
            router_f32 = jnp.pad(p["moe_router"][j].astype(_F32), ((0, 0), (0, LANES - n_experts)))
            router_hi = router_f32.astype(_BF16)
            router_lo = (router_f32 - router_hi.astype(_F32)).astype(_BF16)
            router_pad = jnp.concatenate([router_hi, router_lo], axis=1)
            n = nb * s
            t = _row_tile(s)
            cap = -(-n // moe_tile) * moe_tile + moe_tile
            h1, route, pos, cnt = _mixer_b(h, p["b_w_pw1"], row(p["b_b_pw1"][j]), p["b_conv_w"][j],
                                           row(p["b_conv_b"][j]), row(p["b_ln_g"][j]),
                                           row(p["b_ln_b"][j]), p["b_w_pw2"], row(p["b_b_pw2"][j]),
                                           row(p["ln_mix_g"][i]), row(p["ln_mix_b"][i]), router_pad,
                                           alpha, n_experts, cap, j)
            h1r = h1.reshape(n, d)
            pos = pos.reshape(n // t, 8, pos.shape[-1])
            picks = (pos[:, 0:1, :], pos[:, 1:2, :])
            counts = cnt[0, :n_experts].astype(jnp.int32)
            xs = _dispatch(h1r, picks, counts, t, cap, moe_tile, n_experts)
            plan = _tile_plan(counts, n_experts, moe_tile, cap, (2 * n) // moe_tile + n_experts)
            fe = p["moe_w_gate"].shape[-1]
            ys = _moe_ffn(xs, plan, p["moe_w_gate"], p["moe_w_up"], p["moe_w_down"],
                          moe_tile, fe // 2, j)
            last = i == depth - 1
            h = _combine(h1r, ys, route.reshape(n, LANES), picks, row(p["ln_ffn_g"][i]),
                         row(p["ln_ffn_b"][i]), alpha, t, (nb, seq) if last else None)
            if last:
                return h
            h = h.reshape(nb, s, d)
    return h[:, N_META:]


_MATMUL_WEIGHTS = ("a_w_in", "a_w_out", "b_w_pw1", "b_w_pw2", "ffn_w_gate", "ffn_w_up", "ffn_w_down",
                   "moe_w_gate", "moe_w_up", "moe_w_down")


def kernel(x_prompt, x_sample, meta_tokens, a_w_in, a_conv_w, a_w_out, b_w_pw1, b_b_pw1, b_conv_w,
           b_conv_b, b_ln_g, b_ln_b, b_w_pw2, b_b_pw2, ffn_w_gate, ffn_w_up, ffn_w_down, moe_router,
           moe_w_gate, moe_w_up, moe_w_down, ln_mix_g, ln_mix_b, ln_ffn_g, ln_ffn_b):
    p = dict(meta_tokens=meta_tokens, a_w_in=a_w_in, a_conv_w=a_conv_w, a_w_out=a_w_out,
             b_w_pw1=b_w_pw1, b_b_pw1=b_b_pw1, b_conv_w=b_conv_w, b_conv_b=b_conv_b, b_ln_g=b_ln_g,
             b_ln_b=b_ln_b, b_w_pw2=b_w_pw2, b_b_pw2=b_b_pw2, ffn_w_gate=ffn_w_gate,
             ffn_w_up=ffn_w_up, ffn_w_down=ffn_w_down, moe_router=moe_router, moe_w_gate=moe_w_gate,
             moe_w_up=moe_w_up, moe_w_down=moe_w_down, ln_mix_g=ln_mix_g, ln_mix_b=ln_mix_b,
             ln_ffn_g=ln_ffn_g, ln_ffn_b=ln_ffn_b)
    for name in _MATMUL_WEIGHTS:
        p[name] = p[name].astype(_BF16)
    return (_encode(x_prompt, p, MOE_ROW_TILE), _encode(x_sample, p, MOE_ROW_TILE))
```

```python
import functools

import jax
import jax.numpy as jnp
from jax import lax
from jax.experimental import pallas as pl
from jax.experimental.pallas import tpu as pltpu

N_META = 16
LN_EPS = 1e-5
HALO = 16
LANES = 128
BF16_ROWS = 16
MAX_ROW_TILE = 1024
MOE_ROW_TILE = 512
CONV_SLAB_GROUP = 6
VMEM_LIMIT = 56 * 1024 * 1024

_BF16 = jnp.bfloat16
_F32 = jnp.float32


def _row_tile(rows):
    best = None
    for t in range(BF16_ROWS, MAX_ROW_TILE + 1, BF16_ROWS):
        if rows % t == 0:
            best = t
    assert best is not None, rows
    return best


def _chunks(total, size):
    out, s = [], 0
    while s < total:
        n = min(size, total - s)
        out.append((s, n))
        s += n
    return out


def _dot(a, b):
    return jnp.dot(a, b, preferred_element_type=_F32)


def _layer_norm(x, g, b):
    mu = jnp.mean(x, axis=-1, keepdims=True)
    xc = x - mu
    var = jnp.mean(xc * xc, axis=-1, keepdims=True)
    return xc * lax.rsqrt(var + LN_EPS) * g + b


def _silu(x):
    return x * jax.nn.sigmoid(x)


def _const_spec(shape):
    nd = len(shape)
    return pl.BlockSpec(shape, lambda *_: (0,) * nd, pipeline_mode=pl.Buffered(1))


def _layer_spec(layer, shape, col=0):
    return pl.BlockSpec((None,) + shape, lambda *_: (layer, 0, col), pipeline_mode=pl.Buffered(1))


def _seq_specs(t, d, n_halo_blocks):
    per_tile = t // HALO
    main = pl.BlockSpec((None, t, d), lambda b, i: (b, i, 0))
    prev = pl.BlockSpec((None, HALO, d), lambda b, i: (b, jnp.maximum(i * per_tile - 1, 0), 0))
    nxt = pl.BlockSpec((None, HALO, d),
                       lambda b, i: (b, jnp.minimum((i + 1) * per_tile, n_halo_blocks - 1), 0))
    return prev, main, nxt


def _fill_xbuf(xbuf, hp_ref, hm_ref, hn_ref, t):
    xbuf[0:HALO, :] = hp_ref[...].astype(_BF16)
    xbuf[HALO:HALO + t, :] = hm_ref[...].astype(_BF16)
    xbuf[HALO + t:HALO + t + HALO, :] = hn_ref[...].astype(_BF16)


def _halo_row_mask(t, d):
    i = pl.program_id(1)
    last = pl.num_programs(1) - 1
    r = lax.broadcasted_iota(jnp.int32, (t + 2 * HALO, d), 0)
    return jnp.logical_and(jnp.logical_or(r >= HALO, i > 0),
                           jnp.logical_or(r < HALO + t, i < last))


def _mixer_a_kernel(hp_ref, hm_ref, hn_ref, wb_ref, wc_ref, wh_ref, cw_ref, wo_ref,
                    g_ref, b_ref, out_ref, xbuf, pbuf, *, alpha):
    t, d = hm_ref.shape
    _fill_xbuf(xbuf, hp_ref, hm_ref, hn_ref, t)
    xa = xbuf[...]
    p = _dot(xa, wc_ref[...]) * _dot(xa, wh_ref[...])
    pbuf[...] = jnp.where(_halo_row_mask(t, d), p, 0.0)
    k = cw_ref.shape[0]
    conv = None
    for j in range(k):
        s = HALO - k // 2 + j
        term = cw_ref[j:j + 1, :] * pbuf[s:s + t, :]
        conv = term if conv is None else conv + term
    gate = _dot(xbuf[HALO:HALO + t, :], wb_ref[...])
    mix = _dot((gate * conv).astype(_BF16), wo_ref[...])
    out_ref[...] = _layer_norm(alpha * hm_ref[...] + mix, g_ref[...], b_ref[...])


def _mixer_a(h, w_in, conv_w, w_out, ln_g, ln_b, alpha, layer):
    nb, s, d = h.shape
    t = _row_tile(s)
    prev, main, nxt = _seq_specs(t, d, s // HALO)
    k = conv_w.shape[0]
    w_spec = lambda j: _layer_spec(layer, (d, d), j)
    return pl.pallas_call(
        functools.partial(_mixer_a_kernel, alpha=alpha),
        grid=(nb, s // t),
        in_specs=[prev, main, nxt, w_spec(0), w_spec(1), w_spec(2), _const_spec((k, d)),
                  w_spec(0), _const_spec((1, d)), _const_spec((1, d))],
        out_specs=pl.BlockSpec((None, t, d), lambda b, i: (b, i, 0)),
        out_shape=jax.ShapeDtypeStruct((nb, s, d), _F32),
        scratch_shapes=[pltpu.VMEM((t + 2 * HALO, d), _BF16), pltpu.VMEM((t + 2 * HALO, d), _F32)],
        compiler_params=pltpu.CompilerParams(
            dimension_semantics=("parallel", "parallel"), vmem_limit_bytes=VMEM_LIMIT),
        name="mixer_a",
    )(h, h, h, w_in, w_in, w_in, conv_w, w_out, ln_g, ln_b)


def _ffn_kernel(h_ref, wg_ref, wu_ref, wd_ref, g_ref, b_ref, out_ref, *, alpha, f_chunk):
    h = h_ref[...]
    x = h.astype(_BF16)
    acc = None
    for s, n in _chunks(wg_ref.shape[1], f_chunk):
        gate = _dot(x, wg_ref[:, s:s + n])
        up = _dot(x, wu_ref[:, s:s + n])
        part = _dot((_silu(gate) * up).astype(_BF16), wd_ref[s:s + n, :])
        acc = part if acc is None else acc + part
    out_ref[...] = _layer_norm(alpha * h + acc, g_ref[...], b_ref[...])


def _ffn(h, w_gate, w_up, w_down, ln_g, ln_b, alpha, layer):
    nb, s, d = h.shape
    f = w_gate.shape[2]
    t = _row_tile(s)
    return pl.pallas_call(
        functools.partial(_ffn_kernel, alpha=alpha, f_chunk=1024),
        grid=(nb, s // t),
        in_specs=[pl.BlockSpec((None, t, d), lambda b, i: (b, i, 0)),
                  _layer_spec(layer, (d, f)), _layer_spec(layer, (d, f)), _layer_spec(layer, (f, d)),
                  _const_spec((1, d)), _const_spec((1, d))],
        out_specs=pl.BlockSpec((None, t, d), lambda b, i: (b, i, 0)),
        out_shape=jax.ShapeDtypeStruct((nb, s, d), _F32),
        compiler_params=pltpu.CompilerParams(
            dimension_semantics=("parallel", "parallel"), vmem_limit_bytes=VMEM_LIMIT),
        name="ffn_dense",
    )(h, w_gate, w_up, w_down, ln_g, ln_b)


def _depthwise_conv(ubuf, wbc, cbuf, t, d, k):
    off0 = HALO - k // 2
    taps = {}
    for kk in range(k):
        a, b = divmod(off0 + kk, 8)
        taps.setdefault(b, []).append((a, kk))
    bs = sorted(taps)
    a_all = sorted({a for lst in taps.values() for a, _ in lst})
    assert t % 8 == 0 and (t + 2 * HALO) // 8 >= t // 8 + a_all[-1] + 1
    n_slab = t // 8
    group = min(CONV_SLAB_GROUP, n_slab)
    rows = lax.broadcasted_iota(jnp.int32, (8, LANES), 0)

    def slab_start(m):
        return m * 8 if isinstance(m, int) else pl.multiple_of(m * 8, 8)

    def tree_sum(vals):
        while len(vals) > 1:
            vals = [vals[i] + vals[i + 1] for i in range(0, len(vals) - 1, 2)] + vals[len(vals) & ~1:]
        return vals[0]

    for c in range(d // LANES):
        lanes = pl.ds(c * LANES, LANES)

        def zslab(m, lanes=lanes):
            u = {a: ubuf[pl.ds(slab_start(m + a), 8), lanes] for a in a_all}
            return tuple(tree_sum([wbc[kk * 8:(kk + 1) * 8, lanes] * u[a] for a, kk in taps[b]])
                         for b in bs)

        def combine(zp, zn):
            return tree_sum([p if b == 0 else pltpu.roll(jnp.where(rows < b, n, p), 8 - b, 0)
                             for b, p, n in zip(bs, zp, zn)])

        def slabs(j0, count, zp, lanes=lanes):
            for g in range(count):
                zn = zslab(j0 + g + 1)
                cbuf[pl.ds(slab_start(j0 + g), 8), lanes] = combine(zp, zn)
                zp = zn
            return zp

        n_loop = n_slab // group
        zp = lax.fori_loop(0, n_loop, lambda i, z: slabs(i * group, group, z), zslab(0))
        slabs(n_loop * group, n_slab - n_loop * group, zp)


def _mixer_b_kernel(hp_ref, hm_ref, hn_ref, wa_ref, wg_ref, b1_ref, cw_ref, cb_ref, lg_ref, lb_ref,
                    w2_ref, b2_ref, g_ref, b_ref, rt_ref, out_ref, route_ref, pos_ref, cnt_ref,
                    xbuf, ubuf, cbuf, wbc, pmat, *, alpha, n_experts, cap):
    t, d = hm_ref.shape
    tp = pmat.shape[0]
    _fill_xbuf(xbuf, hp_ref, hm_ref, hn_ref, t)
    xa = xbuf[...]
    a = _dot(xa, wa_ref[...]) + b1_ref[:, 0:d]
    gl = _dot(xa, wg_ref[...]) + b1_ref[:, d:2 * d]
    ubuf[...] = jnp.where(_halo_row_mask(t, d), a * jax.nn.sigmoid(gl), 0.0)
    k = cw_ref.shape[0]
    for kk in range(k):
        wbc[kk * 8:(kk + 1) * 8, :] = jnp.broadcast_to(cw_ref[kk:kk + 1, :], (8, d))
    _depthwise_conv(ubuf, wbc, cbuf, t, d, k)
    u = _silu(_layer_norm(cbuf[...] + cb_ref[...], lg_ref[...], lb_ref[...]))
    mix = _dot(u.astype(_BF16), w2_ref[...]) + b2_ref[...]
    h1 = _layer_norm(alpha * hm_ref[...] + mix, g_ref[...], b_ref[...])
    out_ref[...] = h1

    hi = h1.astype(_BF16)
    lo = (h1 - hi.astype(_F32)).astype(_BF16)
    p_hi = _dot(hi, rt_ref[...])
    logits = p_hi[:, 0:LANES] + p_hi[:, LANES:2 * LANES] + _dot(lo, rt_ref[:, 0:LANES])
    lane = lax.broadcasted_iota(jnp.int32, logits.shape, 1).astype(_F32)
    neg = jnp.float32(-jnp.inf)
    lg1 = jnp.where(lane < n_experts, logits, neg)
    m1 = jnp.max(lg1, axis=-1, keepdims=True)
    i1 = jnp.min(jnp.where(lg1 == m1, lane, float(LANES)), axis=-1, keepdims=True)
    lg2 = jnp.where(lane == i1, neg, lg1)
    m2 = jnp.max(lg2, axis=-1, keepdims=True)
    i2 = jnp.min(jnp.where(lg2 == m2, lane, float(LANES)), axis=-1, keepdims=True)
    e2 = jnp.exp(m2 - m1)
    den = 1.0 + e2
    route = jnp.where(lane == 0.0, i1, jnp.where(lane == 1.0, i2,
                      jnp.where(lane == 2.0, 1.0 / den, jnp.where(lane == 3.0, e2 / den, 0.0))))
    route_ref[...] = route

    @pl.when(jnp.logical_and(pl.program_id(0) == 0, pl.program_id(1) == 0))
    def _():
        cnt_ref[...] = jnp.zeros_like(cnt_ref)

    member = jnp.where(jnp.logical_or(lane == i1, lane == i2), 1.0, 0.0)
    if tp > t:
        pmat[t:tp, :] = jnp.zeros((tp - t, LANES), _F32)
    pmat[0:t, :] = member
    earlier = (lax.broadcasted_iota(jnp.int32, (t, tp), 1) < lax.broadcasted_iota(jnp.int32, (t, tp), 0))
    before = _dot(jnp.where(earlier, 1.0, 0.0).astype(_BF16), pmat[...].astype(_BF16))
    slot = before + cnt_ref[...] + lane * float(cap)
    pos1 = jnp.sum(jnp.where(lane == i1, slot, 0.0), axis=-1, keepdims=True)
    pos2 = jnp.sum(jnp.where(lane == i2, slot, 0.0), axis=-1, keepdims=True)
    cnt_ref[...] += jnp.sum(member, axis=0, keepdims=True)
    pmat[0:t, :] = jnp.where(lane == 0.0, pos1, jnp.where(lane == 1.0, pos2, 0.0))
    pos_ref[...] = pmat[...].T[0:8, :].astype(jnp.int32)


def _mixer_b(h, w_pw1, b_pw1, conv_w, conv_b, cln_g, cln_b, w_pw2, b_pw2, ln_g, ln_b, router_pad,
             alpha, n_experts, cap, layer):
    nb, s, d = h.shape
    t = _row_tile(s)
    tp = -(-t // LANES) * LANES
    prev, main, nxt = _seq_specs(t, d, s // HALO)
    k = conv_w.shape[0]
    w_spec = lambda j: _layer_spec(layer, (d, d), j)
    vec = _const_spec((1, d))
    return pl.pallas_call(
        functools.partial(_mixer_b_kernel, alpha=alpha, n_experts=n_experts, cap=cap),
        grid=(nb, s // t),
        in_specs=[prev, main, nxt, w_spec(0), w_spec(1), _const_spec((1, 2 * d)),
                  _const_spec((k, d)), vec, vec, vec, w_spec(0), vec, vec, vec,
                  _const_spec((d, 2 * LANES))],
        out_specs=[pl.BlockSpec((None, t, d), lambda b, i: (b, i, 0)),
                   pl.BlockSpec((None, t, LANES), lambda b, i: (b, i, 0)),
                   pl.BlockSpec((None, None, 8, tp), lambda b, i: (b, i, 0, 0)),
                   pl.BlockSpec((1, LANES), lambda b, i: (0, 0))],
        out_shape=[jax.ShapeDtypeStruct((nb, s, d), _F32),
                   jax.ShapeDtypeStruct((nb, s, LANES), _F32),
                   jax.ShapeDtypeStruct((nb, s // t, 8, tp), jnp.int32),
                   jax.ShapeDtypeStruct((1, LANES), _F32)],
        scratch_shapes=[pltpu.VMEM((t + 2 * HALO, d), _BF16), pltpu.VMEM((t + 2 * HALO, d), _F32),
                        pltpu.VMEM((t, d), _F32), pltpu.VMEM((8 * k, d), _F32),
                        pltpu.VMEM((tp, LANES), _F32)],
        compiler_params=pltpu.CompilerParams(
            dimension_semantics=("arbitrary", "arbitrary"), vmem_limit_bytes=VMEM_LIMIT),
        name="mixer_b",
    )(h, h, h, w_pw1, w_pw1, b_pw1, conv_w, conv_b, cln_g, cln_b, w_pw2, b_pw2, ln_g, ln_b,
      router_pad)


ROWS_PER_ISSUE = 8


def _rows_to_tiles(dst_ref, x):
    n, d = x.shape
    c = d // LANES
    for j in range(c):
        dst_ref[pl.ds(j, n, stride=c), :] = x[:, j * LANES:(j + 1) * LANES]


def _tiles_to_rows(src_ref, n, c):
    return jnp.concatenate([src_ref[pl.ds(j, n, stride=c), :] for j in range(c)], axis=1)


def _token_tile(ref, token, c):
    start = token * c
    if c % 8 == 0:
        start = pl.multiple_of(start, 8)
    return ref.at[pl.ds(start, c), :]


def _dispatch_kernel(cnt_ref, pos0_ref, pos1_ref, h_ref, xs_hbm, tbuf, zbuf, sem, zsem,
                     *, cap, tm, n_experts):
    i = pl.program_id(0)
    last = pl.num_programs(0) - 1
    slot = i % 2
    t, d = h_ref.shape
    c = d // LANES

    def zero_block(e):
        start = (e * cap + cnt_ref[e]) * c
        if c % 8 == 0:
            start = pl.multiple_of(start, 8)
        return pltpu.make_async_copy(zbuf, xs_hbm.at[pl.ds(start, tm * c), :], zsem)

    def wait_step(s):
        for _ in range(2):
            pltpu.make_async_copy(tbuf.at[s], tbuf.at[s], sem.at[s]).wait()

    @pl.when(i == 0)
    def _():
        zbuf[...] = jnp.zeros_like(zbuf)
        for e in range(n_experts):
            zero_block(e).start()

    @pl.when(i >= 2)
    def _():
        wait_step(slot)

    _rows_to_tiles(tbuf.at[slot], h_ref[...])

    def body(q, carry):
        for u in range(ROWS_PER_ISSUE):
            r = q * ROWS_PER_ISSUE + u
            for k, p_ref in enumerate((pos0_ref, pos1_ref)):
                pltpu.make_async_copy(_token_tile(tbuf.at[slot], r, c),
                                      _token_tile(xs_hbm, p_ref[0, 0, r], c), sem.at[slot]).start(priority=k)
        return carry
    lax.fori_loop(0, t // ROWS_PER_ISSUE, body, 0)

    @pl.when(i == last)
    def _():
        wait_step(slot)

        @pl.when(i >= 1)
        def _():
            wait_step(1 - slot)
        for e in range(n_experts):
            zero_block(e).wait()


def _dispatch(h_rows, picks, counts, t, cap, tm, n_experts):
    n, d = h_rows.shape
    c = d // LANES
    n_tiles, _, tp = picks[0].shape
    assert n_tiles * t == n and t % ROWS_PER_ISSUE == 0 and c * LANES == d
    pick_spec = pl.BlockSpec((1, 1, tp), lambda i, cnt: (i, 0, 0), memory_space=pltpu.SMEM)
    grid_spec = pltpu.PrefetchScalarGridSpec(
        num_scalar_prefetch=1,
        grid=(n_tiles,),
        in_specs=[pick_spec, pick_spec, pl.BlockSpec((t, d), lambda i, cnt: (i, 0))],
        out_specs=pl.BlockSpec(memory_space=pl.ANY),
        scratch_shapes=[pltpu.VMEM((2, t * c, LANES), _F32), pltpu.VMEM((tm * c, LANES), _F32),
                        pltpu.SemaphoreType.DMA((2,)), pltpu.SemaphoreType.DMA(())],
    )
    return pl.pallas_call(
        functools.partial(_dispatch_kernel, cap=cap, tm=tm, n_experts=n_experts),
        grid_spec=grid_spec,
        out_shape=jax.ShapeDtypeStruct((n_experts * cap * c, LANES), _F32),
        compiler_params=pltpu.CompilerParams(
            dimension_semantics=("arbitrary",), vmem_limit_bytes=VMEM_LIMIT),
        name="moe_dispatch",
    )(counts, picks[0], picks[1], h_rows)


def _moe_kernel(texp_ref, tblk_ref, nact_ref, x_ref, wg_ref, wu_ref, wd_ref, y_ref, *, f_chunk):
    del texp_ref, tblk_ref
    c = wg_ref.shape[0] // LANES
    tm = x_ref.shape[0] // c

    @pl.when(pl.program_id(0) < nact_ref[0])
    def _():
        x = _tiles_to_rows(x_ref, tm, c).astype(_BF16)
        acc = None
        for s, n in _chunks(wg_ref.shape[1], f_chunk):
            gate = _dot(x, wg_ref[:, s:s + n])
            up = _dot(x, wu_ref[:, s:s + n])
            part = _dot((_silu(gate) * up).astype(_BF16), wd_ref[s:s + n, :])
            acc = part if acc is None else acc + part
        _rows_to_tiles(y_ref, acc)


def _tile_plan(counts, n_experts, tm, cap, m_tiles):
    tiles = (counts + tm - 1) // tm
    ends = jnp.cumsum(tiles)
    n_act = ends[-1]
    m = jnp.minimum(jnp.arange(m_tiles, dtype=jnp.int32), n_act - 1)
    te = jnp.minimum(jnp.sum((m[:, None] >= ends[None, :]).astype(jnp.int32), axis=1), n_experts - 1)
    first = jnp.sum(jnp.where(te[:, None] == jnp.arange(n_experts, dtype=jnp.int32)[None, :],
                              (ends - tiles)[None, :], 0), axis=1)
    tb = te * (cap // tm) + (m - first)
    return te.astype(jnp.int32), tb.astype(jnp.int32), n_act.reshape(1).astype(jnp.int32)


def _moe_ffn(xs, plan, w_gate, w_up, w_down, tm, f_chunk, layer):
    _, _, d, fe = w_gate.shape
    c = d // LANES
    te, tb, n_act = plan
    m_tiles = te.shape[0]
    w_spec = lambda shape: pl.BlockSpec((None, None) + shape, lambda m, te, tb, na: (layer, te[m], 0, 0),
                                        pipeline_mode=pl.Buffered(1))
    grid_spec = pltpu.PrefetchScalarGridSpec(
        num_scalar_prefetch=3,
        grid=(m_tiles,),
        in_specs=[pl.BlockSpec((tm * c, LANES), lambda m, te, tb, na: (tb[m], 0)),
                  w_spec((d, fe)), w_spec((d, fe)), w_spec((fe, d))],
        out_specs=pl.BlockSpec((tm * c, LANES), lambda m, te, tb, na: (tb[m], 0)),
    )
    return pl.pallas_call(
        functools.partial(_moe_kernel, f_chunk=f_chunk),
        grid_spec=grid_spec,
        out_shape=jax.ShapeDtypeStruct(xs.shape, _F32),
        compiler_params=pltpu.CompilerParams(
            dimension_semantics=("arbitrary",), vmem_limit_bytes=VMEM_LIMIT),
        name="moe_ffn",
    )(te, tb, n_act, xs, w_gate, w_up, w_down)


def _combine_kernel(pos0_ref, pos1_ref, pos0n_ref, pos1n_ref, h_ref, route_ref, g_ref, b_ref, y_hbm,
                    out_ref, gbuf_even, gbuf_odd, sem, *scratch, alpha, tiles_per_seq):
    i = pl.program_id(0)
    n = pl.num_programs(0)
    t, d = h_ref.shape
    c = d // LANES
    slot = i % 2
    final = tiles_per_seq is not None
    chunk = min(ROWS_PER_ISSUE, t)
    gbufs = (gbuf_even, gbuf_odd)

    def issue(p_refs, s, r0, rows):
        for u in range(rows):
            for k, p_ref in enumerate(p_refs):
                pltpu.make_async_copy(_token_tile(y_hbm, p_ref[0, 0, r0 + u], c),
                                      _token_tile(gbufs[s].at[k], r0 + u, c), sem.at[s]).start(priority=k)

    def loop(fn):
        n_loop = t // chunk

        def body(q, carry):
            fn(pl.multiple_of(q * chunk, 8), chunk)
            return carry
        lax.fori_loop(0, n_loop, body, 0)
        if t > n_loop * chunk:
            fn(n_loop * chunk, t - n_loop * chunk)

    if final:
        obuf, sem_head, sem_body = scratch

        def writeback(step, s):
            b, j = step // tiles_per_seq, step % tiles_per_seq
            head = pltpu.make_async_copy(
                obuf.at[s, pl.ds(0, N_META), :],
                out_ref.at[b, pl.ds(pl.multiple_of(jnp.maximum(j * t - N_META, 0), 8), N_META), :],
                sem_head.at[s])
            body = pltpu.make_async_copy(
                obuf.at[s, pl.ds(N_META, t - N_META), :],
                out_ref.at[b, pl.ds(pl.multiple_of(j * t, 8), t - N_META), :], sem_body.at[s])
            return j > 0, head, body

        def wait_writeback(step, s):
            has_head, head, body = writeback(step, s)
            body.wait()

            @pl.when(has_head)
            def _():
                head.wait()

    @pl.when(i == 0)
    def _():
        loop(lambda r0, rows: issue((pos0_ref, pos1_ref), 0, r0, rows))

    def compute(cur, dst):
        y = [_tiles_to_rows(cur.at[k], t, c) for k in range(2)]
        ff = route_ref[:, 2:3] * y[0] + route_ref[:, 3:4] * y[1]
        dst[...] = _layer_norm(alpha * h_ref[...] + ff, g_ref[...], b_ref[...])

    for par in range(2):
        @pl.when(slot == par)
        def _(par=par):
            cur = gbufs[par]
            dst = obuf.at[par] if final else out_ref
            pltpu.make_async_copy(cur, cur, sem.at[par]).wait()
            if final:
                @pl.when(i >= 2)
                def _():
                    wait_writeback(i - 2, par)

            @pl.when(i + 1 < n)
            def _():
                issue((pos0n_ref, pos1n_ref), 1 - par, 0, t)
                compute(cur, dst)

            @pl.when(i + 1 >= n)
            def _():
                compute(cur, dst)

    if final:
        has_head, head, body = writeback(i, slot)
        body.start()

        @pl.when(has_head)
        def _():
            head.start()

        @pl.when(i == n - 1)
        def _():
            wait_writeback(i, slot)

            @pl.when(i >= 1)
            def _():
                wait_writeback(i - 1, 1 - slot)


def _combine(h_rows, ys, route, picks, ln_g, ln_b, alpha, t, final_shape=None):
    n, d = h_rows.shape
    n_tiles, _, tp = picks[0].shape
    assert n_tiles * t == n and t % ROWS_PER_ISSUE == 0 and t > N_META
    pick_spec = lambda fn: pl.BlockSpec((1, 1, tp), fn, memory_space=pltpu.SMEM)
    this_tile = pick_spec(lambda i: (i, 0, 0))
    next_tile = pick_spec(lambda i: (jnp.minimum(i + 1, n_tiles - 1), 0, 0))
    gather_buf = pltpu.VMEM((2, t * (d // LANES), LANES), _F32)
    scratch = [gather_buf, gather_buf, pltpu.SemaphoreType.DMA((2,))]
    if final_shape is None:
        tiles_per_seq = None
        out_specs = pl.BlockSpec((t, d), lambda i: (i, 0))
        out_shape = jax.ShapeDtypeStruct((n, d), _F32)
    else:
        nb, seq = final_shape
        tiles_per_seq = (seq + N_META) // t
        assert nb * tiles_per_seq == n_tiles
        out_specs = pl.BlockSpec(memory_space=pl.ANY)
        out_shape = jax.ShapeDtypeStruct((nb, seq, d), _F32)
        scratch += [pltpu.VMEM((2, t, d), _F32), pltpu.SemaphoreType.DMA((2,)),
                    pltpu.SemaphoreType.DMA((2,))]
    return pl.pallas_call(
        functools.partial(_combine_kernel, alpha=alpha, tiles_per_seq=tiles_per_seq),
        grid=(n_tiles,),
        in_specs=[this_tile, this_tile, next_tile, next_tile,
                  pl.BlockSpec((t, d), lambda i: (i, 0)),
                  pl.BlockSpec((t, LANES), lambda i: (i, 0)),
                  _const_spec((1, d)), _const_spec((1, d)),
                  pl.BlockSpec(memory_space=pl.ANY)],
        out_specs=out_specs,
        out_shape=out_shape,
        scratch_shapes=scratch,
        compiler_params=pltpu.CompilerParams(
            dimension_semantics=("arbitrary",), vmem_limit_bytes=VMEM_LIMIT),
        name="moe_combine",
    )(picks[0], picks[1], picks[0], picks[1], h_rows, route, ln_g, ln_b, ys)


def _encode(x, p, moe_tile):
    nb, seq, d = x.shape
    depth = p["ln_mix_g"].shape[0]
    alpha = (2.0 * depth) ** 0.25
    n_experts = p["moe_router"].shape[-1]
    meta = jnp.broadcast_to(p["meta_tokens"].astype(x.dtype)[None], (nb, N_META, d))
    h = lax.dynamic_update_slice(jnp.pad(x, ((0, 0), (N_META, 0), (0, 0))), meta, (0, 0, 0))
    s = h.shape[1]
    row = lambda v: v.reshape(1, -1)
    for i in range(depth):
        j = i // 2
        if i % 2 == 0:
            h = _mixer_a(h, p["a_w_in"], p["a_conv_w"][j], p["a_w_out"],
                         row(p["ln_mix_g"][i]), row(p["ln_mix_b"][i]), alpha, j)
            h = _ffn(h, p["ffn_w_gate"], p["ffn_w_up"], p["ffn_w_down"],
                     row(p["ln_ffn_g"][i]), row(p["ln_ffn_b"][i]), alpha, j)
        else:
            router_f32 = jnp.pad(p["moe_router"][j].astype(_F32), ((0, 0), (0, LANES - n_experts)))
            router_hi = router_f32.astype(_BF16)
            router_lo = (router_f32 - router_hi.astype(_F32)).astype(_BF16)
            router_pad = jnp.concatenate([router_hi, router_lo], axis=1)
            n = nb * s
            t = _row_tile(s)
            cap = -(-n // moe_tile) * moe_tile + moe_tile
            h1, route, pos, cnt = _mixer_b(h, p["b_w_pw1"], row(p["b_b_pw1"][j]), p["b_conv_w"][j],
                                           row(p["b_conv_b"][j]), row(p["b_ln_g"][j]),
                                           row(p["b_ln_b"][j]), p["b_w_pw2"], row(p["b_b_pw2"][j]),
                                           row(p["ln_mix_g"][i]), row(p["ln_mix_b"][i]), router_pad,
                                           alpha, n_experts, cap, j)
            h1r = h1.reshape(n, d)
            pos = pos.reshape(n // t, 8, pos.shape[-1])
            picks = (pos[:, 0:1, :], pos[:, 1:2, :])
            counts = cnt[0, :n_experts].astype(jnp.int32)
            xs = _dispatch(h1r, picks, counts, t, cap, moe_tile, n_experts)
            plan = _tile_plan(counts, n_experts, moe_tile, cap, (2 * n) // moe_tile + n_experts)
            fe = p["moe_w_gate"].shape[-1]
            ys = _moe_ffn(xs, plan, p["moe_w_gate"], p["moe_w_up"], p["moe_w_down"],
                          moe_tile, fe // 2, j)
            last = i == depth - 1
            h = _combine(h1r, ys, route.reshape(n, LANES), picks, row(p["ln_ffn_g"][i]),
                         row(p["ln_ffn_b"][i]), alpha, t, (nb, seq) if last else None)
            if last:
                return h
            h = h.reshape(nb, s, d)
    return h[:, N_META:]


_MATMUL_WEIGHTS = ("a_w_in", "a_w_out", "b_w_pw1", "b_w_pw2", "ffn_w_gate", "ffn_w_up", "ffn_w_down",
                   "moe_w_gate", "moe_w_up", "moe_w_down")


def kernel(x_prompt, x_sample, meta_tokens, a_w_in, a_conv_w, a_w_out, b_w_pw1, b_b_pw1, b_conv_w,
           b_conv_b, b_ln_g, b_ln_b, b_w_pw2, b_b_pw2, ffn_w_gate, ffn_w_up, ffn_w_down, moe_router,
           moe_w_gate, moe_w_up, moe_w_down, ln_mix_g, ln_mix_b, ln_ffn_g, ln_ffn_b):
    p = dict(meta_tokens=meta_tokens, a_w_in=a_w_in, a_conv_w=a_conv_w, a_w_out=a_w_out,
             b_w_pw1=b_w_pw1, b_b_pw1=b_b_pw1, b_conv_w=b_conv_w, b_conv_b=b_conv_b, b_ln_g=b_ln_g,
             b_ln_b=b_ln_b, b_w_pw2=b_w_pw2, b_b_pw2=b_b_pw2, ffn_w_gate=ffn_w_gate,
             ffn_w_up=ffn_w_up, ffn_w_down=ffn_w_down, moe_router=moe_router, moe_w_gate=moe_w_gate,
             moe_w_up=moe_w_up, moe_w_down=moe_w_down, ln_mix_g=ln_mix_g, ln_mix_b=ln_mix_b,
             ln_ffn_g=ln_ffn_g, ln_ffn_b=ln_ffn_b)
    for name in _MATMUL_WEIGHTS:
        p[name] = p[name].astype(_BF16)
    return (_encode(x_prompt, p, MOE_ROW_TILE), _encode(x_sample, p, MOE_ROW_TILE))
```

```python
import functools

import jax
import jax.numpy as jnp
from jax import lax
from jax.experimental import pallas as pl
from jax.experimental.pallas import tpu as pltpu

N_META = 16
LN_EPS = 1e-5
HALO = 16
LANES = 128
BF16_ROWS = 16
MAX_ROW_TILE = 1024
MOE_ROW_TILE = 512
MOE_F_CHUNK = 512
CONV_SLAB_GROUP = 6
VMEM_LIMIT = 56 * 1024 * 1024

_BF16 = jnp.bfloat16
_F32 = jnp.float32


def _row_tile(rows):
    best = None
    for t in range(BF16_ROWS, MAX_ROW_TILE + 1, BF16_ROWS):
        if rows % t == 0:
            best = t
    assert best is not None, rows
    return best


def _chunks(total, size):
    out, s = [], 0
    while s < total:
        n = min(size, total - s)
        out.append((s, n))
        s += n
    return out


def _dot(a, b):
    return jnp.dot(a, b, preferred_element_type=_F32)


def _layer_norm(x, g, b):
    mu = jnp.mean(x, axis=-1, keepdims=True)
    xc = x - mu
    var = jnp.mean(xc * xc, axis=-1, keepdims=True)
    return xc * lax.rsqrt(var + LN_EPS) * g + b


def _silu(x):
    return x * jax.nn.sigmoid(x)


def _const_spec(shape):
    nd = len(shape)
    return pl.BlockSpec(shape, lambda *_: (0,) * nd, pipeline_mode=pl.Buffered(1))


def _layer_spec(layer, shape, col=0):
    return pl.BlockSpec((None,) + shape, lambda *_: (layer, 0, col), pipeline_mode=pl.Buffered(1))


def _seq_specs(t, d, n_halo_blocks):
    per_tile = t // HALO
    main = pl.BlockSpec((None, t, d), lambda b, i: (b, i, 0))
    prev = pl.BlockSpec((None, HALO, d), lambda b, i: (b, jnp.maximum(i * per_tile - 1, 0), 0))
    nxt = pl.BlockSpec((None, HALO, d),
                       lambda b, i: (b, jnp.minimum((i + 1) * per_tile, n_halo_blocks - 1), 0))
    return prev, main, nxt


def _fill_xbuf(xbuf, hp_ref, hm_ref, hn_ref, t):
    xbuf[0:HALO, :] = hp_ref[...].astype(_BF16)
    xbuf[HALO:HALO + t, :] = hm_ref[...].astype(_BF16)
    xbuf[HALO + t:HALO + t + HALO, :] = hn_ref[...].astype(_BF16)


def _halo_row_mask(t, d):
    i = pl.program_id(1)
    last = pl.num_programs(1) - 1
    r = lax.broadcasted_iota(jnp.int32, (t + 2 * HALO, d), 0)
    return jnp.logical_and(jnp.logical_or(r >= HALO, i > 0),
                           jnp.logical_or(r < HALO + t, i < last))


def _mixer_a_kernel(hp_ref, hm_ref, hn_ref, wb_ref, wc_ref, wh_ref, cw_ref, wo_ref,
                    g_ref, b_ref, out_ref, xbuf, pbuf, *, alpha):
    t, d = hm_ref.shape
    _fill_xbuf(xbuf, hp_ref, hm_ref, hn_ref, t)
    xa = xbuf[...]
    p = _dot(xa, wc_ref[...]) * _dot(xa, wh_ref[...])
    pbuf[...] = jnp.where(_halo_row_mask(t, d), p, 0.0)
    k = cw_ref.shape[0]
    conv = None
    for j in range(k):
        s = HALO - k // 2 + j
        term = cw_ref[j:j + 1, :] * pbuf[s:s + t, :]
        conv = term if conv is None else conv + term
    gate = _dot(xbuf[HALO:HALO + t, :], wb_ref[...])
    mix = _dot((gate * conv).astype(_BF16), wo_ref[...])
    out_ref[...] = _layer_norm(alpha * hm_ref[...] + mix, g_ref[...], b_ref[...])


def _mixer_a(h, w_in, conv_w, w_out, ln_g, ln_b, alpha, layer):
    nb, s, d = h.shape
    t = _row_tile(s)
    prev, main, nxt = _seq_specs(t, d, s // HALO)
    k = conv_w.shape[0]
    w_spec = lambda j: _layer_spec(layer, (d, d), j)
    return pl.pallas_call(
        functools.partial(_mixer_a_kernel, alpha=alpha),
        grid=(nb, s // t),
        in_specs=[prev, main, nxt, w_spec(0), w_spec(1), w_spec(2), _const_spec((k, d)),
                  w_spec(0), _const_spec((1, d)), _const_spec((1, d))],
        out_specs=pl.BlockSpec((None, t, d), lambda b, i: (b, i, 0)),
        out_shape=jax.ShapeDtypeStruct((nb, s, d), _F32),
        scratch_shapes=[pltpu.VMEM((t + 2 * HALO, d), _BF16), pltpu.VMEM((t + 2 * HALO, d), _F32)],
        compiler_params=pltpu.CompilerParams(
            dimension_semantics=("parallel", "parallel"), vmem_limit_bytes=VMEM_LIMIT),
        name="mixer_a",
    )(h, h, h, w_in, w_in, w_in, conv_w, w_out, ln_g, ln_b)


def _ffn_kernel(h_ref, wg_ref, wu_ref, wd_ref, g_ref, b_ref, out_ref, *, alpha, f_chunk):
    h = h_ref[...]
    x = h.astype(_BF16)
    acc = None
    for s, n in _chunks(wg_ref.shape[1], f_chunk):
        gate = _dot(x, wg_ref[:, s:s + n])
        up = _dot(x, wu_ref[:, s:s + n])
        part = _dot((_silu(gate) * up).astype(_BF16), wd_ref[s:s + n, :])
        acc = part if acc is None else acc + part
    out_ref[...] = _layer_norm(alpha * h + acc, g_ref[...], b_ref[...])


def _ffn(h, w_gate, w_up, w_down, ln_g, ln_b, alpha, layer):
    nb, s, d = h.shape
    f = w_gate.shape[2]
    t = _row_tile(s)
    return pl.pallas_call(
        functools.partial(_ffn_kernel, alpha=alpha, f_chunk=1024),
        grid=(nb, s // t),
        in_specs=[pl.BlockSpec((None, t, d), lambda b, i: (b, i, 0)),
                  _layer_spec(layer, (d, f)), _layer_spec(layer, (d, f)), _layer_spec(layer, (f, d)),
                  _const_spec((1, d)), _const_spec((1, d))],
        out_specs=pl.BlockSpec((None, t, d), lambda b, i: (b, i, 0)),
        out_shape=jax.ShapeDtypeStruct((nb, s, d), _F32),
        compiler_params=pltpu.CompilerParams(
            dimension_semantics=("parallel", "parallel"), vmem_limit_bytes=VMEM_LIMIT),
        name="ffn_dense",
    )(h, w_gate, w_up, w_down, ln_g, ln_b)


def _depthwise_conv(ubuf, wbc, cbuf, t, d, k):
    off0 = HALO - k // 2
    taps = {}
    for kk in range(k):
        a, b = divmod(off0 + kk, 8)
        taps.setdefault(b, []).append((a, kk))
    bs = sorted(taps)
    a_all = sorted({a for lst in taps.values() for a, _ in lst})
    assert t % 8 == 0 and (t + 2 * HALO) // 8 >= t // 8 + a_all[-1] + 1
    n_slab = t // 8
    group = min(CONV_SLAB_GROUP, n_slab)
    rows = lax.broadcasted_iota(jnp.int32, (8, LANES), 0)

    def slab_start(m):
        return m * 8 if isinstance(m, int) else pl.multiple_of(m * 8, 8)

    def tree_sum(vals):
        while len(vals) > 1:
            vals = [vals[i] + vals[i + 1] for i in range(0, len(vals) - 1, 2)] + vals[len(vals) & ~1:]
        return vals[0]

    for c in range(d // LANES):
        lanes = pl.ds(c * LANES, LANES)

        def zslab(m, lanes=lanes):
            u = {a: ubuf[pl.ds(slab_start(m + a), 8), lanes] for a in a_all}
            return tuple(tree_sum([wbc[kk * 8:(kk + 1) * 8, lanes] * u[a] for a, kk in taps[b]])
                         for b in bs)

        def combine(zp, zn):
            return tree_sum([p if b == 0 else pltpu.roll(jnp.where(rows < b, n, p), 8 - b, 0)
                             for b, p, n in zip(bs, zp, zn)])

        def slabs(j0, count, zp, lanes=lanes):
            for g in range(count):
                zn = zslab(j0 + g + 1)
                cbuf[pl.ds(slab_start(j0 + g), 8), lanes] = combine(zp, zn)
                zp = zn
            return zp

        n_loop = n_slab // group
        zp = lax.fori_loop(0, n_loop, lambda i, z: slabs(i * group, group, z), zslab(0))
        slabs(n_loop * group, n_slab - n_loop * group, zp)


def _mixer_b_kernel(hp_ref, hm_ref, hn_ref, wa_ref, wg_ref, b1_ref, cw_ref, cb_ref, lg_ref, lb_ref,
                    w2_ref, b2_ref, g_ref, b_ref, rt_ref, out_ref, route_ref, pos_ref, cnt_ref,
                    xbuf, ubuf, cbuf, wbc, pmat, *, alpha, n_experts, cap):
    t, d = hm_ref.shape
    tp = pmat.shape[0]
    _fill_xbuf(xbuf, hp_ref, hm_ref, hn_ref, t)
    xa = xbuf[...]
    a = _dot(xa, wa_ref[...]) + b1_ref[:, 0:d]
    gl = _dot(xa, wg_ref[...]) + b1_ref[:, d:2 * d]
    ubuf[...] = jnp.where(_halo_row_mask(t, d), a * jax.nn.sigmoid(gl), 0.0)
    k = cw_ref.shape[0]
    for kk in range(k):
        wbc[kk * 8:(kk + 1) * 8, :] = jnp.broadcast_to(cw_ref[kk:kk + 1, :], (8, d))
    _depthwise_conv(ubuf, wbc, cbuf, t, d, k)
    u = _silu(_layer_norm(cbuf[...] + cb_ref[...], lg_ref[...], lb_ref[...]))
    mix = _dot(u.astype(_BF16), w2_ref[...]) + b2_ref[...]
    h1 = _layer_norm(alpha * hm_ref[...] + mix, g_ref[...], b_ref[...])
    out_ref[...] = h1

    hi = h1.astype(_BF16)
    lo = (h1 - hi.astype(_F32)).astype(_BF16)
    p_hi = _dot(hi, rt_ref[...])
    logits = p_hi[:, 0:LANES] + p_hi[:, LANES:2 * LANES] + _dot(lo, rt_ref[:, 0:LANES])
    lane = lax.broadcasted_iota(jnp.int32, logits.shape, 1).astype(_F32)
    neg = jnp.float32(-jnp.inf)
    lg1 = jnp.where(lane < n_experts, logits, neg)
    m1 = jnp.max(lg1, axis=-1, keepdims=True)
    i1 = jnp.min(jnp.where(lg1 == m1, lane, float(LANES)), axis=-1, keepdims=True)
    lg2 = jnp.where(lane == i1, neg, lg1)
    m2 = jnp.max(lg2, axis=-1, keepdims=True)
    i2 = jnp.min(jnp.where(lg2 == m2, lane, float(LANES)), axis=-1, keepdims=True)
    e2 = jnp.exp(m2 - m1)
    den = 1.0 + e2
    route = jnp.where(lane == 0.0, i1, jnp.where(lane == 1.0, i2,
                      jnp.where(lane == 2.0, 1.0 / den, jnp.where(lane == 3.0, e2 / den, 0.0))))
    route_ref[...] = route

    @pl.when(jnp.logical_and(pl.program_id(0) == 0, pl.program_id(1) == 0))
    def _():
        cnt_ref[...] = jnp.zeros_like(cnt_ref)

    member = jnp.where(jnp.logical_or(lane == i1, lane == i2), 1.0, 0.0)
    if tp > t:
        pmat[t:tp, :] = jnp.zeros((tp - t, LANES), _F32)
    pmat[0:t, :] = member
    earlier = (lax.broadcasted_iota(jnp.int32, (t, tp), 1) < lax.broadcasted_iota(jnp.int32, (t, tp), 0))
    before = _dot(jnp.where(earlier, 1.0, 0.0).astype(_BF16), pmat[...].astype(_BF16))
    slot = before + cnt_ref[...] + lane * float(cap)
    pos1 = jnp.sum(jnp.where(lane == i1, slot, 0.0), axis=-1, keepdims=True)
    pos2 = jnp.sum(jnp.where(lane == i2, slot, 0.0), axis=-1, keepdims=True)
    cnt_ref[...] += jnp.sum(member, axis=0, keepdims=True)
    pmat[0:t, :] = jnp.where(lane == 0.0, pos1, jnp.where(lane == 1.0, pos2, 0.0))
    pos_ref[...] = pmat[...].T[0:8, :].astype(jnp.int32)


def _mixer_b(h, w_pw1, b_pw1, conv_w, conv_b, cln_g, cln_b, w_pw2, b_pw2, ln_g, ln_b, router_pad,
             alpha, n_experts, cap, layer):
    nb, s, d = h.shape
    t = _row_tile(s)
    tp = -(-t // LANES) * LANES
    prev, main, nxt = _seq_specs(t, d, s // HALO)
    k = conv_w.shape[0]
    w_spec = lambda j: _layer_spec(layer, (d, d), j)
    vec = _const_spec((1, d))
    return pl.pallas_call(
        functools.partial(_mixer_b_kernel, alpha=alpha, n_experts=n_experts, cap=cap),
        grid=(nb, s // t),
        in_specs=[prev, main, nxt, w_spec(0), w_spec(1), _const_spec((1, 2 * d)),
                  _const_spec((k, d)), vec, vec, vec, w_spec(0), vec, vec, vec,
                  _const_spec((d, 2 * LANES))],
        out_specs=[pl.BlockSpec((None, t, d), lambda b, i: (b, i, 0)),
                   pl.BlockSpec((None, t, LANES), lambda b, i: (b, i, 0)),
                   pl.BlockSpec((None, None, 8, tp), lambda b, i: (b, i, 0, 0)),
                   pl.BlockSpec((1, LANES), lambda b, i: (0, 0))],
        out_shape=[jax.ShapeDtypeStruct((nb, s, d), _F32),
                   jax.ShapeDtypeStruct((nb, s, LANES), _F32),
                   jax.ShapeDtypeStruct((nb, s // t, 8, tp), jnp.int32),
                   jax.ShapeDtypeStruct((1, LANES), _F32)],
        scratch_shapes=[pltpu.VMEM((t + 2 * HALO, d), _BF16), pltpu.VMEM((t + 2 * HALO, d), _F32),
                        pltpu.VMEM((t, d), _F32), pltpu.VMEM((8 * k, d), _F32),
                        pltpu.VMEM((tp, LANES), _F32)],
        compiler_params=pltpu.CompilerParams(
            dimension_semantics=("arbitrary", "arbitrary"), vmem_limit_bytes=VMEM_LIMIT),
        name="mixer_b",
    )(h, h, h, w_pw1, w_pw1, b_pw1, conv_w, conv_b, cln_g, cln_b, w_pw2, b_pw2, ln_g, ln_b,
      router_pad)


ROWS_PER_ISSUE = 8


def _rows_to_tiles(dst_ref, x):
    n, d = x.shape
    c = d // LANES
    for j in range(c):
        dst_ref[pl.ds(j, n, stride=c), :] = x[:, j * LANES:(j + 1) * LANES]


def _tiles_to_rows(src_ref, n, c):
    return jnp.concatenate([src_ref[pl.ds(j, n, stride=c), :] for j in range(c)], axis=1)


def _token_tile(ref, token, c):
    start = token * c
    if c % 8 == 0:
        start = pl.multiple_of(start, 8)
    return ref.at[pl.ds(start, c), :]


def _dispatch_kernel(cnt_ref, pos0_ref, pos1_ref, h_ref, xs_hbm, tbuf, zbuf, sem, zsem,
                     *, cap, tm, n_experts):
    i = pl.program_id(0)
    last = pl.num_programs(0) - 1
    slot = i % 2
    t, d = h_ref.shape
    c = d // LANES

    def zero_block(e):
        start = (e * cap + cnt_ref[e]) * c
        if c % 8 == 0:
            start = pl.multiple_of(start, 8)
        return pltpu.make_async_copy(zbuf, xs_hbm.at[pl.ds(start, tm * c), :], zsem)

    def wait_step(s):
        for _ in range(2):
            pltpu.make_async_copy(tbuf.at[s], tbuf.at[s], sem.at[s]).wait()

    @pl.when(i == 0)
    def _():
        zbuf[...] = jnp.zeros_like(zbuf)
        for e in range(n_experts):
            zero_block(e).start()

    @pl.when(i >= 2)
    def _():
        wait_step(slot)

    _rows_to_tiles(tbuf.at[slot], h_ref[...])

    def body(q, carry):
        for u in range(ROWS_PER_ISSUE):
            r = q * ROWS_PER_ISSUE + u
            for k, p_ref in enumerate((pos0_ref, pos1_ref)):
                pltpu.make_async_copy(_token_tile(tbuf.at[slot], r, c),
                                      _token_tile(xs_hbm, p_ref[0, 0, r], c), sem.at[slot]).start(priority=k)
        return carry
    lax.fori_loop(0, t // ROWS_PER_ISSUE, body, 0)

    @pl.when(i == last)
    def _():
        wait_step(slot)

        @pl.when(i >= 1)
        def _():
            wait_step(1 - slot)
        for e in range(n_experts):
            zero_block(e).wait()


def _dispatch(h_rows, picks, counts, t, cap, tm, n_experts):
    n, d = h_rows.shape
    c = d // LANES
    n_tiles, _, tp = picks[0].shape
    assert n_tiles * t == n and t % ROWS_PER_ISSUE == 0 and c * LANES == d
    pick_spec = pl.BlockSpec((1, 1, tp), lambda i, cnt: (i, 0, 0), memory_space=pltpu.SMEM)
    grid_spec = pltpu.PrefetchScalarGridSpec(
        num_scalar_prefetch=1,
        grid=(n_tiles,),
        in_specs=[pick_spec, pick_spec, pl.BlockSpec((t, d), lambda i, cnt: (i, 0))],
        out_specs=pl.BlockSpec(memory_space=pl.ANY),
        scratch_shapes=[pltpu.VMEM((2, t * c, LANES), _F32), pltpu.VMEM((tm * c, LANES), _F32),
                        pltpu.SemaphoreType.DMA((2,)), pltpu.SemaphoreType.DMA(())],
    )
    return pl.pallas_call(
        functools.partial(_dispatch_kernel, cap=cap, tm=tm, n_experts=n_experts),
        grid_spec=grid_spec,
        out_shape=jax.ShapeDtypeStruct((n_experts * cap * c, LANES), _F32),
        compiler_params=pltpu.CompilerParams(
            dimension_semantics=("arbitrary",), vmem_limit_bytes=VMEM_LIMIT),
        name="moe_dispatch",
    )(counts, picks[0], picks[1], h_rows)


def _moe_kernel(texp_ref, tblk_ref, nact_ref, x_ref, wg_hbm, wu_hbm, wd_hbm, y_ref,
                wg_buf, wu_buf, wd_buf, sem, *, layer):
    del tblk_ref
    m = pl.program_id(0)
    n_chunks, d, f_chunk = wg_buf.shape
    c = d // LANES
    tm = x_ref.shape[0] // c
    last_step = pl.num_programs(0) - 1
    expert = texp_ref[m]
    next_expert = texp_ref[jnp.minimum(m + 1, last_step)]
    active = m < nact_ref[0]
    run_start = jnp.logical_or(m == 0, texp_ref[jnp.maximum(m - 1, 0)] != expert)
    expert_changes = next_expert != expert

    def chunk_copies(e, j):
        cols = pl.ds(j * f_chunk, f_chunk)
        return (pltpu.make_async_copy(wg_hbm.at[layer, e, :, cols], wg_buf.at[j], sem.at[j]),
                pltpu.make_async_copy(wu_hbm.at[layer, e, :, cols], wu_buf.at[j], sem.at[j]),
                pltpu.make_async_copy(wd_hbm.at[layer, e, cols, :], wd_buf.at[j], sem.at[j]))

    @pl.when(jnp.logical_and(active, m == 0))
    def _():
        for j in range(n_chunks):
            for cp in chunk_copies(expert, j):
                cp.start()

    @pl.when(jnp.logical_and(active, run_start))
    def _():
        for j in range(n_chunks):
            for cp in chunk_copies(expert, j):
                cp.wait()

    def tile(refill):
        x = _tiles_to_rows(x_ref, tm, c).astype(_BF16)
        acc = None
        for j in range(n_chunks):
            gate = _dot(x, wg_buf[j])
            up = _dot(x, wu_buf[j])
            part = _dot((_silu(gate) * up).astype(_BF16), wd_buf[j])
            acc = part if acc is None else acc + part
            if refill:
                for cp in chunk_copies(next_expert, j):
                    cp.start()
        _rows_to_tiles(y_ref, acc)

    @pl.when(jnp.logical_and(active, expert_changes))
    def _():
        tile(True)

    @pl.when(jnp.logical_and(active, jnp.logical_not(expert_changes)))
    def _():
        tile(False)


def _tile_plan(counts, n_experts, tm, cap, m_tiles):
    tiles = (counts + tm - 1) // tm
    ends = jnp.cumsum(tiles)
    n_act = ends[-1]
    m = jnp.minimum(jnp.arange(m_tiles, dtype=jnp.int32), n_act - 1)
    te = jnp.minimum(jnp.sum((m[:, None] >= ends[None, :]).astype(jnp.int32), axis=1), n_experts - 1)
    first = jnp.sum(jnp.where(te[:, None] == jnp.arange(n_experts, dtype=jnp.int32)[None, :],
                              (ends - tiles)[None, :], 0), axis=1)
    tb = te * (cap // tm) + (m - first)
    return te.astype(jnp.int32), tb.astype(jnp.int32), n_act.reshape(1).astype(jnp.int32)


def _moe_ffn(xs, plan, w_gate, w_up, w_down, tm, f_chunk, layer):
    _, _, d, fe = w_gate.shape
    c = d // LANES
    te, tb, n_act = plan
    m_tiles = te.shape[0]
    n_chunks = fe // f_chunk
    assert n_chunks * f_chunk == fe
    in_hbm = pl.BlockSpec(memory_space=pl.ANY)
    grid_spec = pltpu.PrefetchScalarGridSpec(
        num_scalar_prefetch=3,
        grid=(m_tiles,),
        in_specs=[pl.BlockSpec((tm * c, LANES), lambda m, te, tb, na: (tb[m], 0)),
                  in_hbm, in_hbm, in_hbm],
        out_specs=pl.BlockSpec((tm * c, LANES), lambda m, te, tb, na: (tb[m], 0)),
        scratch_shapes=[pltpu.VMEM((n_chunks, d, f_chunk), _BF16), pltpu.VMEM((n_chunks, d, f_chunk), _BF16),
                        pltpu.VMEM((n_chunks, f_chunk, d), _BF16), pltpu.SemaphoreType.DMA((n_chunks,))],
    )
    return pl.pallas_call(
        functools.partial(_moe_kernel, layer=layer),
        grid_spec=grid_spec,
        out_shape=jax.ShapeDtypeStruct(xs.shape, _F32),
        compiler_params=pltpu.CompilerParams(
            dimension_semantics=("arbitrary",), vmem_limit_bytes=VMEM_LIMIT),
        name="moe_ffn",
    )(te, tb, n_act, xs, w_gate, w_up, w_down)


def _combine_kernel(pos0_ref, pos1_ref, pos0n_ref, pos1n_ref, h_ref, route_ref, g_ref, b_ref, y_hbm,
                    out_ref, gbuf_even, gbuf_odd, sem, *scratch, alpha, tiles_per_seq):
    i = pl.program_id(0)
    n = pl.num_programs(0)
    t, d = h_ref.shape
    c = d // LANES
    slot = i % 2
    final = tiles_per_seq is not None
    chunk = min(ROWS_PER_ISSUE, t)
    gbufs = (gbuf_even, gbuf_odd)

    def issue(p_refs, s, r0, rows):
        for u in range(rows):
            for k, p_ref in enumerate(p_refs):
                pltpu.make_async_copy(_token_tile(y_hbm, p_ref[0, 0, r0 + u], c),
                                      _token_tile(gbufs[s].at[k], r0 + u, c), sem.at[s]).start(priority=k)

    def loop(fn):
        n_loop = t // chunk

        def body(q, carry):
            fn(pl.multiple_of(q * chunk, 8), chunk)
            return carry
        lax.fori_loop(0, n_loop, body, 0)
        if t > n_loop * chunk:
            fn(n_loop * chunk, t - n_loop * chunk)

    if final:
        obuf, sem_head, sem_body = scratch

        def writeback(step, s):
            b, j = step // tiles_per_seq, step % tiles_per_seq
            head = pltpu.make_async_copy(
                obuf.at[s, pl.ds(0, N_META), :],
                out_ref.at[b, pl.ds(pl.multiple_of(jnp.maximum(j * t - N_META, 0), 8), N_META), :],
                sem_head.at[s])
            body = pltpu.make_async_copy(
                obuf.at[s, pl.ds(N_META, t - N_META), :],
                out_ref.at[b, pl.ds(pl.multiple_of(j * t, 8), t - N_META), :], sem_body.at[s])
            return j > 0, head, body

        def wait_writeback(step, s):
            has_head, head, body = writeback(step, s)
            body.wait()

            @pl.when(has_head)
            def _():
                head.wait()

    @pl.when(i == 0)
    def _():
        loop(lambda r0, rows: issue((pos0_ref, pos1_ref), 0, r0, rows))

    def compute(cur, dst):
        y = [_tiles_to_rows(cur.at[k], t, c) for k in range(2)]
        ff = route_ref[:, 2:3] * y[0] + route_ref[:, 3:4] * y[1]
        dst[...] = _layer_norm(alpha * h_ref[...] + ff, g_ref[...], b_ref[...])

    for par in range(2):
        @pl.when(slot == par)
        def _(par=par):
            cur = gbufs[par]
            dst = obuf.at[par] if final else out_ref
            pltpu.make_async_copy(cur, cur, sem.at[par]).wait()
            if final:
                @pl.when(i >= 2)
                def _():
                    wait_writeback(i - 2, par)

            @pl.when(i + 1 < n)
            def _():
                issue((pos0n_ref, pos1n_ref), 1 - par, 0, t)
                compute(cur, dst)

            @pl.when(i + 1 >= n)
            def _():
                compute(cur, dst)

    if final:
        has_head, head, body = writeback(i, slot)
        body.start()

        @pl.when(has_head)
        def _():
            head.start()

        @pl.when(i == n - 1)
        def _():
            wait_writeback(i, slot)

            @pl.when(i >= 1)
            def _():
                wait_writeback(i - 1, 1 - slot)


def _combine(h_rows, ys, route, picks, ln_g, ln_b, alpha, t, final_shape=None):
    n, d = h_rows.shape
    n_tiles, _, tp = picks[0].shape
    assert n_tiles * t == n and t % ROWS_PER_ISSUE == 0 and t > N_META
    pick_spec = lambda fn: pl.BlockSpec((1, 1, tp), fn, memory_space=pltpu.SMEM)
    this_tile = pick_spec(lambda i: (i, 0, 0))
    next_tile = pick_spec(lambda i: (jnp.minimum(i + 1, n_tiles - 1), 0, 0))
    gather_buf = pltpu.VMEM((2, t * (d // LANES), LANES), _F32)
    scratch = [gather_buf, gather_buf, pltpu.SemaphoreType.DMA((2,))]
    if final_shape is None:
        tiles_per_seq = None
        out_specs = pl.BlockSpec((t, d), lambda i: (i, 0))
        out_shape = jax.ShapeDtypeStruct((n, d), _F32)
    else:
        nb, seq = final_shape
        tiles_per_seq = (seq + N_META) // t
        assert nb * tiles_per_seq == n_tiles
        out_specs = pl.BlockSpec(memory_space=pl.ANY)
        out_shape = jax.ShapeDtypeStruct((nb, seq, d), _F32)
        scratch += [pltpu.VMEM((2, t, d), _F32), pltpu.SemaphoreType.DMA((2,)),
                    pltpu.SemaphoreType.DMA((2,))]
    return pl.pallas_call(
        functools.partial(_combine_kernel, alpha=alpha, tiles_per_seq=tiles_per_seq),
        grid=(n_tiles,),
        in_specs=[this_tile, this_tile, next_tile, next_tile,
                  pl.BlockSpec((t, d), lambda i: (i, 0)),
                  pl.BlockSpec((t, LANES), lambda i: (i, 0)),
                  _const_spec((1, d)), _const_spec((1, d)),
                  pl.BlockSpec(memory_space=pl.ANY)],
        out_specs=out_specs,
        out_shape=out_shape,
        scratch_shapes=scratch,
        compiler_params=pltpu.CompilerParams(
            dimension_semantics=("arbitrary",), vmem_limit_bytes=VMEM_LIMIT),
        name="moe_combine",
    )(picks[0], picks[1], picks[0], picks[1], h_rows, route, ln_g, ln_b, ys)


def _encode(x, p, moe_tile):
    nb, seq, d = x.shape
    depth = p["ln_mix_g"].shape[0]
    alpha = (2.0 * depth) ** 0.25
    n_experts = p["moe_router"].shape[-1]
    meta = jnp.broadcast_to(p["meta_tokens"].astype(x.dtype)[None], (nb, N_META, d))
    h = lax.dynamic_update_slice(jnp.pad(x, ((0, 0), (N_META, 0), (0, 0))), meta, (0, 0, 0))
    s = h.shape[1]
    row = lambda v: v.reshape(1, -1)
    for i in range(depth):
        j = i // 2
        if i % 2 == 0:
            h = _mixer_a(h, p["a_w_in"], p["a_conv_w"][j], p["a_w_out"],
                         row(p["ln_mix_g"][i]), row(p["ln_mix_b"][i]), alpha, j)
            h = _ffn(h, p["ffn_w_gate"], p["ffn_w_up"], p["ffn_w_down"],
                     row(p["ln_ffn_g"][i]), row(p["ln_ffn_b"][i]), alpha, j)
        else:
            router_f32 = jnp.pad(p["moe_router"][j].astype(_F32), ((0, 0), (0, LANES - n_experts)))
            router_hi = router_f32.astype(_BF16)
            router_lo = (router_f32 - router_hi.astype(_F32)).astype(_BF16)
            router_pad = jnp.concatenate([router_hi, router_lo], axis=1)
            n = nb * s
            t = _row_tile(s)
            cap = -(-n // moe_tile) * moe_tile + moe_tile
            h1, route, pos, cnt = _mixer_b(h, p["b_w_pw1"], row(p["b_b_pw1"][j]), p["b_conv_w"][j],
                                           row(p["b_conv_b"][j]), row(p["b_ln_g"][j]),
                                           row(p["b_ln_b"][j]), p["b_w_pw2"], row(p["b_b_pw2"][j]),
                                           row(p["ln_mix_g"][i]), row(p["ln_mix_b"][i]), router_pad,
                                           alpha, n_experts, cap, j)
            h1r = h1.reshape(n, d)
            pos = pos.reshape(n // t, 8, pos.shape[-1])
            picks = (pos[:, 0:1, :], pos[:, 1:2, :])
            counts = cnt[0, :n_experts].astype(jnp.int32)
            xs = _dispatch(h1r, picks, counts, t, cap, moe_tile, n_experts)
            plan = _tile_plan(counts, n_experts, moe_tile, cap, (2 * n) // moe_tile + n_experts)
            fe = p["moe_w_gate"].shape[-1]
            ys = _moe_ffn(xs, plan, p["moe_w_gate"], p["moe_w_up"], p["moe_w_down"],
                          moe_tile, min(MOE_F_CHUNK, fe), j)
            last = i == depth - 1
            h = _combine(h1r, ys, route.reshape(n, LANES), picks, row(p["ln_ffn_g"][i]),
                         row(p["ln_ffn_b"][i]), alpha, t, (nb, seq) if last else None)
            if last:
                return h
            h = h.reshape(nb, s, d)
    return h[:, N_META:]


_MATMUL_WEIGHTS = ("a_w_in", "a_w_out", "b_w_pw1", "b_w_pw2", "ffn_w_gate", "ffn_w_up", "ffn_w_down",
                   "moe_w_gate", "moe_w_up", "moe_w_down")


def kernel(x_prompt, x_sample, meta_tokens, a_w_in, a_conv_w, a_w_out, b_w_pw1, b_b_pw1, b_conv_w,
           b_conv_b, b_ln_g, b_ln_b, b_w_pw2, b_b_pw2, ffn_w_gate, ffn_w_up, ffn_w_down, moe_router,
           moe_w_gate, moe_w_up, moe_w_down, ln_mix_g, ln_mix_b, ln_ffn_g, ln_ffn_b):
    p = dict(meta_tokens=meta_tokens, a_w_in=a_w_in, a_conv_w=a_conv_w, a_w_out=a_w_out,
             b_w_pw1=b_w_pw1, b_b_pw1=b_b_pw1, b_conv_w=b_conv_w, b_conv_b=b_conv_b, b_ln_g=b_ln_g,
             b_ln_b=b_ln_b, b_w_pw2=b_w_pw2, b_b_pw2=b_b_pw2, ffn_w_gate=ffn_w_gate,
             ffn_w_up=ffn_w_up, ffn_w_down=ffn_w_down, moe_router=moe_router, moe_w_gate=moe_w_gate,
             moe_w_up=moe_w_up, moe_w_down=moe_w_down, ln_mix_g=ln_mix_g, ln_mix_b=ln_mix_b,
             ln_ffn_g=ln_ffn_g, ln_ffn_b=ln_ffn_b)
    for name in _MATMUL_WEIGHTS:
        p[name] = p[name].astype(_BF16)
    return (_encode(x_prompt, p, MOE_ROW_TILE), _encode(x_sample, p, MOE_ROW_TILE))
```
